```python
import math
import jax, jax.numpy as jnp
from jax import lax
import numpy as np

D_MODEL = 2048
BATCH = 4
SEQ = 2048
DEPTH = 4
DEC_BATCH = 128
DEC_SEQ = 4
PAST_LEN = 8192
PAGE_SIZE = 128

N_MIXERS = 4
NL_A = (DEPTH + 3) // 4
NL_B = (DEPTH + 2) // 4
NL_C = (DEPTH + 1) // 4
NL_D = DEPTH // 4
BLOCK = 128
NEG = -1e30
EPS = 1e-6
HEAD_DIM = 128
N_HEADS = D_MODEL // HEAD_DIM
BRANCH = N_HEADS * HEAD_DIM
N_BUCKETS = 32
T5_MAX_DISTANCE = 2048
WIN_A = 128
KV_A = 2
G_A = N_HEADS // KV_A
Q_LORA = 512
KV_LORA = 512
NOPE_DIM = 128
ROPE_DIM = 64
V_DIM_B = BRANCH // N_HEADS
ROPE_THETA = 10000.0
C_WINDOWS = (128, 512, 2048)
C_DILATIONS = (1, 4, 16)
KV_C = 4
G_C = N_HEADS // KV_C
DH_D = HEAD_DIM // 2
KV_D = 4
G_D = N_HEADS // KV_D
A_IN = N_HEADS * HEAD_DIM + 2 * KV_A * HEAD_DIM + BRANCH
B_IN = Q_LORA + KV_LORA + ROPE_DIM + BRANCH
C_GROUP_IN = N_HEADS * HEAD_DIM + 2 * KV_C * HEAD_DIM
C_IN = len(C_WINDOWS) * C_GROUP_IN + BRANCH
D_IN = N_HEADS * 2 * DH_D + 2 * KV_D * 2 * DH_D + BRANCH

kernel_name = 'hybrid_swa_mla_dilated_diff_decoder_step'


def _rms(x, g):
    x32 = x.astype(jnp.float32)
    y = x32 * lax.rsqrt(jnp.mean(x32 * x32, axis=-1, keepdims=True) + EPS)
    return (y * g.astype(jnp.float32)).astype(x.dtype)


def _proj(u, w):
    return jnp.einsum('btd,dn->btn', u, w)


def _gated_out(o, gate, w_out):
    b, t = gate.shape[:2]
    return jnp.einsum('btc,cd->btd', o.reshape(b, t, BRANCH) * jax.nn.silu(gate), w_out)


def _t5_bias(rel_bias, dist):
    exact = N_BUCKETS // 2
    df = jnp.maximum(dist, 1).astype(jnp.float32)
    far = exact + (jnp.log(df / exact) / math.log(T5_MAX_DISTANCE / exact) * (N_BUCKETS - exact)).astype(jnp.int32)
    bucket = jnp.where(dist < exact, dist, jnp.minimum(far, N_BUCKETS - 1))
    return rel_bias[bucket].astype(jnp.float32)


def _rope(x, pos):
    half = x.shape[-1] // 2
    inv = ROPE_THETA ** (-jnp.arange(half, dtype=jnp.float32) / half)
    ang = pos.astype(jnp.float32)[:, None] * inv[None, :]
    ang = ang.reshape((ang.shape[0],) + (1,) * (x.ndim - 3) + (half,))
    cos, sin = jnp.cos(ang), jnp.sin(ang)
    x1 = x[..., :half].astype(jnp.float32)
    x2 = x[..., half:].astype(jnp.float32)
    return jnp.concatenate([x1 * cos - x2 * sin, x1 * sin + x2 * cos], axis=-1).astype(x.dtype)


def _softmax_stats(s, sink=None):
    full = s
    if sink is not None:
        full = jnp.concatenate([s, jnp.broadcast_to(sink.astype(jnp.float32), s.shape[:-1] + (1,))], axis=-1)
    lse = jax.nn.logsumexp(full, axis=-1, keepdims=True)
    return jnp.exp(s - lse), lse[..., 0]


def _band_attention(q, k, v, window, dil, rel_bias, sink=None):
    b, L, nkv, g, dh = q.shape
    n = L // BLOCK
    qb = q.reshape(b, n, BLOCK, nkv, g, dh)
    kb = k.reshape(b, n, BLOCK, nkv, dh)
    vb = v.reshape(b, n, BLOCK, nkv, dh)
    prev = lambda a: jnp.pad(a[:, :-1], ((0, 0), (1, 0)) + ((0, 0),) * (a.ndim - 2))
    kc = jnp.concatenate([prev(kb), kb], axis=2)
    vc = jnp.concatenate([prev(vb), vb], axis=2)
    s = jnp.einsum('bnqhgd,bnkhd->bnhgqk', qb, kc).astype(jnp.float32) * dh ** -0.5
    dist = (BLOCK + jnp.arange(BLOCK))[:, None] - jnp.arange(2 * BLOCK)[None, :]
    bias = _t5_bias(rel_bias, jnp.maximum(dist, 0) * dil)
    bias = jnp.transpose(bias, (2, 0, 1)).reshape(nkv, g, BLOCK, 2 * BLOCK)
    first = (jnp.arange(n) == 0)[:, None, None] & (jnp.arange(2 * BLOCK) < BLOCK)[None, None, :]
    mask = ((dist >= 0) & (dist <= window))[None] & ~first
    s = jnp.where(mask[None, :, None, None], s + bias, NEG)
    p, lse = _softmax_stats(s, sink)
    o = jnp.einsum('bnhgqk,bnkhd->bnqhgd', p.astype(v.dtype), vc).reshape(b, L, nkv, g, dh)
    lse = jnp.transpose(lse, (0, 1, 4, 2, 3)).reshape(b, L, nkv, g)
    return o, lse


def _window_decode(q, k_all, v_all, n_past, window, dil, rel_bias, sink=None):
    b, S, nkv, g, dh = q.shape
    steps = jnp.arange(window // dil + 1) * dil
    idx = n_past + jnp.arange(S)[:, None] - steps[None, :]
    valid = idx >= 0
    idx = jnp.maximum(idx, 0)
    kg = k_all[:, idx]
    vg = v_all[:, idx]
    s = jnp.einsum('bshgd,bskhd->bshgk', q, kg).astype(jnp.float32) * dh ** -0.5
    bias = _t5_bias(rel_bias, steps).T.reshape(nkv, g, -1)
    s = jnp.where(valid[None, :, None, None, :], s + bias, NEG)
    p, lse = _softmax_stats(s, sink)
    o = jnp.einsum('bshgk,bskhd->bshgd', p.astype(vg.dtype), vg)
    return o, lse


def _a_split(u, w_in):
    z = _proj(u, w_in)
    b, t = u.shape[:2]
    nq, nk = N_HEADS * HEAD_DIM, KV_A * HEAD_DIM
    q = z[..., :nq].reshape(b, t, KV_A, G_A, HEAD_DIM)
    kv = z[..., nq:nq + 2 * nk].reshape(b, t, 2, KV_A, HEAD_DIM)
    return q, kv, z[..., nq + 2 * nk:]


def _a_prompt(u, w_in, sink, w_out, rel_bias):
    q, kv, gate = _a_split(u, w_in)
    o, _ = _band_attention(q, kv[:, :, 0], kv[:, :, 1], WIN_A, 1, rel_bias, sink.reshape(KV_A, G_A, 1, 1))
    return _gated_out(o, gate, w_out), kv[:, -min(WIN_A, u.shape[1]):]


def _a_sample(u, buf, w_in, sink, w_out, rel_bias):
    q, kv, gate = _a_split(u, w_in)
    n_past = buf.shape[1]
    kv_all = jnp.concatenate([buf, kv], axis=1)
    o, _ = _window_decode(q, kv_all[:, :, 0], kv_all[:, :, 1], n_past, WIN_A, 1, rel_bias, sink.reshape(KV_A, G_A, 1))
    return _gated_out(o, gate, w_out), kv_all[:, -n_past:]


def _b_split(u, pos, w_in, q_norm, kv_norm, w_uq, w_uk):
    z = _proj(u, w_in)
    cq = _rms(z[..., :Q_LORA], q_norm)
    ckv = _rms(z[..., Q_LORA:Q_LORA + KV_LORA], kv_norm)
    kpe = _rope(z[..., Q_LORA + KV_LORA:Q_LORA + KV_LORA + ROPE_DIM], pos)
    gate = z[..., Q_LORA + KV_LORA + ROPE_DIM:]
    q = jnp.einsum('btr,rhe->bthe', cq, w_uq)
    q_lat = jnp.einsum('bthn,rhn->bthr', q[..., :NOPE_DIM], w_uk)
    q_pe = _rope(q[..., NOPE_DIM:], pos)
    return q_lat, q_pe, ckv, kpe, gate


def _mla_core(q_lat, q_pe, ckv, kpe, qpos, kpos):
    s = (jnp.einsum('bshr,blr->bhsl', q_lat, ckv) + jnp.einsum('bshe,ble->bhsl', q_pe, kpe)).astype(jnp.float32)
    s = jnp.where(kpos[None, :] <= qpos[:, None], s * (NOPE_DIM + ROPE_DIM) ** -0.5, NEG)
    p = jax.nn.softmax(s, axis=-1)
    return jnp.einsum('bhsl,blr->bshr', p.astype(ckv.dtype), ckv)


def _b_prompt(u, w_in, q_norm, kv_norm, w_uq, w_uk, w_uv, w_out):
    b, t = u.shape[:2]
    pos = jnp.arange(t)
    q_lat, q_pe, ckv, kpe, gate = _b_split(u, pos, w_in, q_norm, kv_norm, w_uq, w_uk)
    nq = t // BLOCK
    blk = lambda a: jnp.moveaxis(a.reshape((b, nq, BLOCK) + a.shape[2:]), 1, 0)

    def one(args):
        ql, qp, jb = args
        return _mla_core(ql, qp, ckv, kpe, jb * BLOCK + jnp.arange(BLOCK), pos)

    o_lat = lax.map(one, (blk(q_lat), blk(q_pe), jnp.arange(nq)))
    o_lat = jnp.moveaxis(o_lat, 0, 1).reshape(b, t, N_HEADS, KV_LORA)
    o = jnp.einsum('bshr,rhe->bshe', o_lat, w_uv)
    return _gated_out(o, gate, w_out), ckv, kpe


def _b_sample(u, lat_pool, kpe_pool, j, page_table, w_in, q_norm, kv_norm, w_uq, w_uk, w_uv, w_out):
    s_len = u.shape[1]
    past = page_table.shape[1] * PAGE_SIZE
    pos = past + jnp.arange(s_len)
    q_lat, q_pe, ckv, kpe, gate = _b_split(u, pos, w_in, q_norm, kv_norm, w_uq, w_uk)
    kpos = jnp.arange(past + s_len)

    def one(args):
        ql, qp, pt, cn, kn = args
        c = jnp.concatenate([lat_pool[j, pt].reshape(past, KV_LORA), cn], axis=0)
        kp = jnp.concatenate([kpe_pool[j, pt].reshape(past, ROPE_DIM), kn], axis=0)
        return _mla_core(ql[None], qp[None], c[None], kp[None], pos, kpos)[0]

    o_lat = lax.map(one, (q_lat, q_pe, page_table, ckv, kpe))
    o = jnp.einsum('bshr,rhe->bshe', o_lat, w_uv)
    return _gated_out(o, gate, w_out), ckv, kpe


def _c_split(u, w_in):
    z = _proj(u, w_in)
    b, t = u.shape[:2]
    nq = N_HEADS * HEAD_DIM
    groups = []
    for gi in range(len(C_WINDOWS)):
        zg = z[..., gi * C_GROUP_IN:(gi + 1) * C_GROUP_IN]
        groups.append((zg[..., :nq].reshape(b, t, KV_C, G_C, HEAD_DIM),
                       zg[..., nq:].reshape(b, t, 2, KV_C, HEAD_DIM)))
    return groups, z[..., len(C_WINDOWS) * C_GROUP_IN:]


def _combine(outs, lses):
    w = jax.nn.softmax(jnp.stack(lses, 0), axis=0)
    o = jnp.sum(w[..., None] * jnp.stack(outs, 0).astype(jnp.float32), axis=0)
    return o.astype(outs[0].dtype)


def _dilated_prompt(q, k, v, window, dil, rel_bias):
    b, t = q.shape[:2]
    span = dil * BLOCK
    tp = -(-t // span) * span
    L = tp // dil

    def sub(a):
        rest = a.shape[2:]
        a = jnp.pad(a, ((0, 0), (0, tp - t)) + ((0, 0),) * len(rest))
        a = jnp.moveaxis(a.reshape((b, L, dil) + rest), 2, 1)
        return a.reshape((b * dil, L) + rest)

    def unsub(a):
        rest = a.shape[2:]
        a = jnp.moveaxis(a.reshape((b, dil, L) + rest), 1, 2)
        return a.reshape((b, tp) + rest)[:, :t]

    o, lse = _band_attention(sub(q), sub(k), sub(v), window // dil, dil, rel_bias)
    return unsub(o), unsub(lse)


def _c_prompt(u, w_in, w_out, rel_bias):
    groups, gate = _c_split(u, w_in)
    t = u.shape[1]
    outs, lses, states = [], [], []
    for (q, kv), w, d in zip(groups, C_WINDOWS, C_DILATIONS):
        o, lse = _dilated_prompt(q, kv[:, :, 0], kv[:, :, 1], w, d, rel_bias)
        outs.append(o)
        lses.append(lse)
        states.append(kv[:, -min(w, t):])
    return _gated_out(_combine(outs, lses), gate, w_out), states


def _c_sample(u, bufs, w_in, w_out, rel_bias):
    groups, gate = _c_split(u, w_in)
    outs, lses, states = [], [], []
    for (q, kv), buf, w, d in zip(groups, bufs, C_WINDOWS, C_DILATIONS):
        n_past = buf.shape[1]
        kv_all = jnp.concatenate([buf, kv], axis=1)
        o, lse = _window_decode(q, kv_all[:, :, 0], kv_all[:, :, 1], n_past, w, d, rel_bias)
        outs.append(o)
        lses.append(lse)
        states.append(kv_all[:, -n_past:])
    return _gated_out(_combine(outs, lses), gate, w_out), states


def _d_split(u, w_in):
    z = _proj(u, w_in)
    b, t = u.shape[:2]
    nq, nkv = N_HEADS * 2 * DH_D, 2 * KV_D * 2 * DH_D
    q = z[..., :nq].reshape(b, t, KV_D, G_D, 2, DH_D)
    kv = z[..., nq:nq + nkv].reshape(b, t, 2, KV_D, 2 * DH_D)
    return q, kv, z[..., nq + nkv:]


def _d_lambda(lam_p, layer):
    lam_init = 0.8 - 0.6 * math.exp(-0.3 * layer)
    lp = lam_p.astype(jnp.float32)
    lam = jnp.exp(jnp.sum(lp[0] * lp[1])) - jnp.exp(jnp.sum(lp[2] * lp[3])) + lam_init
    return lam, lam_init


def _diff_core(q, kv, qpos, kpos, lam, rel_bias):
    b, L = kv.shape[:2]
    S = q.shape[1]
    k = kv[:, :, 0].reshape(b, L, KV_D, 2, DH_D)
    v = kv[:, :, 1]
    s = jnp.einsum('bshgcd,blhcd->bhgcsl', q, k).astype(jnp.float32) * DH_D ** -0.5
    dist = qpos[:, None] - kpos[None, :]
    bias = jnp.transpose(_t5_bias(rel_bias, jnp.maximum(dist, 0)), (2, 0, 1)).reshape(KV_D, G_D, 1, S, L)
    s = jnp.where(dist >= 0, s + bias, NEG)
    p = jax.nn.softmax(s, axis=-1)
    a = p[:, :, :, 0] - lam * p[:, :, :, 1]
    return jnp.einsum('bhgsl,blhe->bshge', a.astype(v.dtype), v)


def _d_finish(o, subln, lam_init, gate, w_out):
    return _gated_out(_rms(o, subln) * (1.0 - lam_init), gate, w_out)


def _d_prompt(u, layer, w_in, lam_p, subln, w_out, rel_bias):
    b, t = u.shape[:2]
    q, kv, gate = _d_split(u, w_in)
    lam, lam_init = _d_lambda(lam_p, layer)
    nq = t // BLOCK
    kpos = jnp.arange(t)
    qb = jnp.moveaxis(q.reshape((b, nq, BLOCK) + q.shape[2:]), 1, 0)

    def one(args):
        qq, jb = args
        return _diff_core(qq, kv, jb * BLOCK + jnp.arange(BLOCK), kpos, lam, rel_bias)

    o = lax.map(one, (qb, jnp.arange(nq)))
    o = jnp.moveaxis(o, 0, 1).reshape(b, t, KV_D, G_D, 2 * DH_D)
    return _d_finish(o, subln, lam_init, gate, w_out), kv


def _d_sample(u, pool, j, page_table, layer, w_in, lam_p, subln, w_out, rel_bias):
    s_len = u.shape[1]
    past = page_table.shape[1] * PAGE_SIZE
    pos = past + jnp.arange(s_len)
    kpos = jnp.arange(past + s_len)
    q, kv, gate = _d_split(u, w_in)
    lam, lam_init = _d_lambda(lam_p, layer)

    def one(args):
        qq, pt, kvn = args
        kv_all = jnp.concatenate([pool[j, pt].reshape(past, 2, KV_D, 2 * DH_D), kvn], axis=0)
        return _diff_core(qq[None], kv_all[None], pos, kpos, lam, rel_bias)[0]

    o = lax.map(one, (q, page_table, kv))
    return _d_finish(o, subln, lam_init, gate, w_out), kv


def setup_inputs(seed: int = 0) -> dict:
    key = jax.random.key(seed)
    ks = jax.random.split(key, 32)
    n_pages = PAST_LEN // PAGE_SIZE
    n_used = DEC_BATCH * n_pages
    n_phys = n_used + (n_used + 3) // 4

    def nrm(k, shape, scale=1.0):
        return jax.random.normal(k, shape, jnp.float32) * scale

    def gain(k, shape):
        return 1.0 + nrm(k, shape, 0.02)

    sd, sb = D_MODEL ** -0.5, BRANCH ** -0.5
    cb = [min(w, PAST_LEN) for w in C_WINDOWS]
    return {
        'x_prompt': nrm(ks[0], (BATCH, SEQ, D_MODEL)),
        'x_sample': nrm(ks[1], (DEC_BATCH, DEC_SEQ, D_MODEL)),
        'cache_a_kv': nrm(ks[2], (NL_A, DEC_BATCH, min(WIN_A, PAST_LEN), 2, KV_A, HEAD_DIM)),
        'cache_b_lat': nrm(ks[3], (NL_B, n_phys, PAGE_SIZE, KV_LORA)),
        'cache_b_kpe': nrm(ks[4], (NL_B, n_phys, PAGE_SIZE, ROPE_DIM)),
        'cache_c_kv1': nrm(ks[5], (NL_C, DEC_BATCH, cb[0], 2, KV_C, HEAD_DIM)),
        'cache_c_kv2': nrm(ks[6], (NL_C, DEC_BATCH, cb[1], 2, KV_C, HEAD_DIM)),
        'cache_c_kv3': nrm(ks[7], (NL_C, DEC_BATCH, cb[2], 2, KV_C, HEAD_DIM)),
        'cache_d_kv': nrm(ks[8], (NL_D, n_phys, PAGE_SIZE, 2, KV_D, 2 * DH_D)),
        'page_table': jax.random.permutation(ks[9], n_phys)[:n_used].reshape(DEC_BATCH, n_pages).astype(jnp.int32),
        'rel_bias': nrm(ks[10], (N_BUCKETS, N_HEADS), 0.5),
        'ln_gain': gain(ks[11], (DEPTH, D_MODEL)),
        'final_gain': gain(ks[12], (D_MODEL,)),
        'a_w_in': nrm(ks[13], (NL_A, D_MODEL, A_IN), sd),
        'a_sink': nrm(ks[14], (NL_A, N_HEADS), 0.5),
        'a_w_out': nrm(ks[15], (NL_A, BRANCH, D_MODEL), sb),
        'b_w_in': nrm(ks[16], (NL_B, D_MODEL, B_IN), sd),
        'b_q_norm': gain(ks[17], (NL_B, Q_LORA)),
        'b_kv_norm': gain(ks[18], (NL_B, KV_LORA)),
        'b_w_uq': nrm(ks[19], (NL_B, Q_LORA, N_HEADS, NOPE_DIM + ROPE_DIM), Q_LORA ** -0.5),
        'b_w_uk': nrm(ks[20], (NL_B, KV_LORA, N_HEADS, NOPE_DIM), KV_LORA ** -0.5),
        'b_w_uv': nrm(ks[21], (NL_B, KV_LORA, N_HEADS, V_DIM_B), KV_LORA ** -0.5),
        'b_w_out': nrm(ks[22], (NL_B, BRANCH, D_MODEL), sb),
        'c_w_in': nrm(ks[23], (NL_C, D_MODEL, C_IN), sd),
        'c_w_out': nrm(ks[24], (NL_C, BRANCH, D_MODEL), sb),
        'd_w_in': nrm(ks[25], (NL_D, D_MODEL, D_IN), sd),
        'd_lambda': nrm(ks[26], (NL_D, 4, DH_D), 0.1),
        'd_subln': gain(ks[27], (NL_D, 2 * DH_D)),
        'd_w_out': nrm(ks[28], (NL_D, BRANCH, D_MODEL), sb),
    }


def reference(x_prompt, x_sample, cache_a_kv, cache_b_lat, cache_b_kpe, cache_c_kv1, cache_c_kv2, cache_c_kv3,
              cache_d_kv, page_table, rel_bias, ln_gain, final_gain, a_w_in, a_sink, a_w_out, b_w_in, b_q_norm,
              b_kv_norm, b_w_uq, b_w_uk, b_w_uv, b_w_out, c_w_in, c_w_out, d_w_in, d_lambda, d_subln, d_w_out):
    a_p, a_s, bl_p, bl_s, bk_p, bk_s, d_p, d_s = [], [], [], [], [], [], [], []
    c_p, c_s = ([], [], []), ([], [], [])
    hp, hs = x_prompt, x_sample
    for i in range(DEPTH):
        m, j = i % N_MIXERS, i // N_MIXERS
        up, us = _rms(hp, ln_gain[i]), _rms(hs, ln_gain[i])
        if m == 0:
            yp, sp = _a_prompt(up, a_w_in[j], a_sink[j], a_w_out[j], rel_bias)
            ys, ss = _a_sample(us, cache_a_kv[j], a_w_in[j], a_sink[j], a_w_out[j], rel_bias)
            a_p.append(sp)
            a_s.append(ss)
        elif m == 1:
            yp, lp, kp = _b_prompt(up, b_w_in[j], b_q_norm[j], b_kv_norm[j], b_w_uq[j], b_w_uk[j], b_w_uv[j], b_w_out[j])
            ys, ls, ksm = _b_sample(us, cache_b_lat, cache_b_kpe, j, page_table, b_w_in[j], b_q_norm[j], b_kv_norm[j],
                                    b_w_uq[j], b_w_uk[j], b_w_uv[j], b_w_out[j])
            bl_p.append(lp)
            bl_s.append(ls)
            bk_p.append(kp)
            bk_s.append(ksm)
        elif m == 2:
            yp, sps = _c_prompt(up, c_w_in[j], c_w_out[j], rel_bias)
            ys, sss = _c_sample(us, (cache_c_kv1[j], cache_c_kv2[j], cache_c_kv3[j]), c_w_in[j], c_w_out[j], rel_bias)
            for g in range(len(C_WINDOWS)):
                c_p[g].append(sps[g])
                c_s[g].append(sss[g])
        else:
            yp, sp = _d_prompt(up, i, d_w_in[j], d_lambda[j], d_subln[j], d_w_out[j], rel_bias)
            ys, ss = _d_sample(us, cache_d_kv, j, page_table, i, d_w_in[j], d_lambda[j], d_subln[j], d_w_out[j], rel_bias)
            d_p.append(sp)
            d_s.append(ss)
        hp, hs = hp + yp, hs + ys
    y_prompt = _rms(hp, final_gain)
    y_sample = _rms(hs, final_gain)
    return (y_prompt, y_sample,
            jnp.stack(a_p), jnp.stack(a_s),
            jnp.stack(bl_p), jnp.stack(bl_s), jnp.stack(bk_p), jnp.stack(bk_s),
            jnp.stack(c_p[0]), jnp.stack(c_s[0]), jnp.stack(c_p[1]), jnp.stack(c_s[1]),
            jnp.stack(c_p[2]), jnp.stack(c_s[2]),
            jnp.stack(d_p), jnp.stack(d_s))
```

```python
import functools
import math

import jax
import jax.numpy as jnp
from jax import lax
from jax.experimental import pallas as pl
from jax.experimental.pallas import tpu as pltpu

F32 = jnp.float32
BF16 = jnp.bfloat16
NEG = -1e30
EPS = 1e-6

D_MODEL = 2048
HEAD_DIM = 128
N_HEADS = 16
BRANCH = N_HEADS * HEAD_DIM
PAGE = 128
BLOCK = 128
N_BUCKETS = 32
T5_MAX_DISTANCE = 2048
WIN_A, KV_A = 128, 2
Q_LORA, KV_LORA, NOPE_DIM, ROPE_DIM = 512, 512, 128, 64
ROPE_THETA = 10000.0
C_WINDOWS, C_DILATIONS, KV_C = (128, 512, 2048), (1, 4, 16), 4
DH_D, KV_D = 64, 4
D_LAYER = 3
PAGES_PER_STEP = 8
VMEM_LIMIT_BYTES = 56 * 1024 * 1024


def _cparams(*sem):
    return pltpu.CompilerParams(dimension_semantics=sem, vmem_limit_bytes=VMEM_LIMIT_BYTES)


def _pick(n, cands):
    for c in cands:
        if n % c == 0:
            return c
    raise ValueError(f"no tile for {n}")


_ROW_TILES = (512, 256, 128, 64, 32, 16, 8)


def _dot_t(a, b):
    return lax.dot_general(a, b, (((1,), (1,)), ((), ())), preferred_element_type=F32)


def _dot(a, b):
    return jnp.dot(a, b, preferred_element_type=F32)


def _rms_proj_kernel(x_ref, g_ref, w_ref, o_ref, xn_ref):
    @pl.when(pl.program_id(1) == 0)
    def _():
        x = x_ref[...]
        y = x * lax.rsqrt(jnp.mean(x * x, axis=-1, keepdims=True) + EPS) * g_ref[...]
        xn_ref[...] = y.astype(BF16)

    o_ref[...] = _dot(xn_ref[...], w_ref[...])


def _rms_proj(x, g, w):
    m, d = x.shape
    n = w.shape[1]
    tm = _pick(m, _ROW_TILES)
    tn = _pick(n, (1024, 768, 640, 512, 384, 256, 128))
    return pl.pallas_call(
        _rms_proj_kernel,
        grid=(m // tm, n // tn),
        in_specs=[pl.BlockSpec((tm, d), lambda i, j: (i, 0)),
                  pl.BlockSpec((1, d), lambda i, j: (0, 0)),
                  pl.BlockSpec((d, tn), lambda i, j: (0, j))],
        out_specs=pl.BlockSpec((tm, tn), lambda i, j: (i, j)),
        out_shape=jax.ShapeDtypeStruct((m, n), F32),
        scratch_shapes=[pltpu.VMEM((tm, d), BF16)],
        compiler_params=_cparams("parallel", "arbitrary"),
        name="rms_proj",
    )(x, g.reshape(1, d), w)


def _gated_out_kernel(*refs, n_groups):
    h_ref, gate_ref, w_ref = refs[:3]
    o_ref, a_ref = refs[-2], refs[-1]
    parts = refs[3:-2]

    @pl.when(pl.program_id(1) == 0)
    def _():
        if n_groups == 1:
            o = parts[0][...]
        else:
            outs = [r[...] for r in parts[:n_groups]]
            lses = [r[...] for r in parts[n_groups:]]
            m = functools.reduce(jnp.maximum, lses)
            es = [jnp.exp(l - m) for l in lses]
            den = functools.reduce(lambda a, b: a + b, es)
            o = functools.reduce(lambda a, b: a + b, [e * x for e, x in zip(es, outs)]) / den
        gate = gate_ref[...]
        a_ref[...] = (o * (gate * jax.nn.sigmoid(gate))).astype(BF16)

    o_ref[...] = h_ref[...] + _dot(a_ref[...], w_ref[...])


def _gated_out(h, z, outs, lses, w_out):
    m, d = h.shape
    tm = _pick(m, (256, 128, 64, 32, 16, 8))
    tn = 1024
    n_groups = len(outs)
    parts = list(outs) + (list(lses) if n_groups > 1 else [])
    row = pl.BlockSpec((tm, BRANCH), lambda i, j: (i, 0))
    return pl.pallas_call(
        functools.partial(_gated_out_kernel, n_groups=n_groups),
        grid=(m // tm, d // tn),
        in_specs=[pl.BlockSpec((tm, tn), lambda i, j: (i, j)), row,
                  pl.BlockSpec((BRANCH, tn), lambda i, j: (0, j))] + [row] * len(parts),
        out_specs=pl.BlockSpec((tm, tn), lambda i, j: (i, j)),
        out_shape=jax.ShapeDtypeStruct((m, d), F32),
        scratch_shapes=[pltpu.VMEM((tm, BRANCH), BF16)],
        compiler_params=_cparams("parallel", "arbitrary"),
        name="gated_out",
    )(h, z, w_out, *parts)


def _rms_kernel(x_ref, g_ref, o_ref):
    x = x_ref[...]
    o_ref[...] = x * lax.rsqrt(jnp.mean(x * x, axis=-1, keepdims=True) + EPS) * g_ref[...]


def _rms_final(x, g):
    m, d = x.shape
    tm = _pick(m, _ROW_TILES)
    return pl.pallas_call(
        _rms_kernel,
        grid=(m // tm,),
        in_specs=[pl.BlockSpec((tm, d), lambda i: (i, 0)), pl.BlockSpec((1, d), lambda i: (0, 0))],
        out_specs=pl.BlockSpec((tm, d), lambda i: (i, 0)),
        out_shape=jax.ShapeDtypeStruct((m, d), F32),
        compiler_params=_cparams("parallel"),
        name="rms_final",
    )(x, g.reshape(1, d))


def _t5_table(rel_bias, n_dist):
    exact = N_BUCKETS // 2
    dist = jnp.arange(n_dist)
    df = jnp.maximum(dist, 1).astype(F32)
    far = exact + (jnp.log(df / exact) / math.log(T5_MAX_DISTANCE / exact) * (N_BUCKETS - exact)).astype(jnp.int32)
    bucket = jnp.where(dist < exact, dist, jnp.minimum(far, N_BUCKETS - 1))
    return rel_bias[bucket].astype(F32).T


def _hankel(u, rows, cols):
    p = u.shape[-1]
    lead = u.shape[:-1]
    flat = jnp.tile(u, (1,) * len(lead) + (rows + 1,))[..., :rows * (p + 1)]
    return flat.reshape(lead + (rows, p + 1))[..., :cols]


def _band_bias(f, dil, n_kv):
    g = N_HEADS // n_kv
    dd = jnp.arange(-(BLOCK - 1), 2 * BLOCK + 1)
    ok = (dd >= 0) & (dd <= BLOCK)
    val = jnp.where(ok[None, :], f[:, jnp.clip(dd, 0, BLOCK) * dil], NEG)
    t = _hankel(val, BLOCK, 2 * BLOCK)[..., ::-1]
    return t.reshape(n_kv, g * BLOCK, 2 * BLOCK)


def _causal_bias(f, tq, n_kv):
    g = N_HEADS // n_kv
    n_delta = f.shape[1] // tq
    tiles = []
    for delta in range(n_delta):
        dd = delta * tq - (tq - 1) + jnp.arange(2 * tq)
        val = jnp.where((dd >= 0)[None, :], f[:, jnp.clip(dd, 0, f.shape[1] - 1)], NEG)
        tiles.append(_hankel(val, tq, tq)[..., ::-1].reshape(n_kv, g * tq, tq))
    return jnp.stack(tiles)


def _decode_bias(f, n_past, n_new, window, dil, n_kv, n_pad):
    g = N_HEADS // n_kv
    n_dist = f.shape[1]
    dd = jnp.arange(n_dist)
    val = jnp.where(((dd <= window) & (dd % dil == 0))[None, :], f, NEG)
    rev = jnp.concatenate([val[:, ::-1], jnp.full((N_HEADS, n_pad), NEG, F32)], axis=1)
    rows = []
    for i in range(n_new):
        s = n_dist - 1 - n_past - i
        rows.append(rev[:, s:s + n_past + n_pad])
    t = jnp.stack(rows, axis=1)
    t = t.reshape(n_kv, g * n_new, n_past + n_pad)
    new = jnp.where(jnp.arange(n_pad)[None, None, :] < n_new, t[..., n_past:], NEG)
    return t[..., :n_past], new


def _band_kernel(*refs, g, has_sink, want_lse, scale):
    q_ref, kp_ref, kc_ref, vp_ref, vc_ref, bias_ref = refs[:6]
    rest = refs[6:]
    if has_sink:
        sink_ref, rest = rest[0], rest[1:]
    o_ref = rest[0]
    q = q_ref[...]
    qs = jnp.concatenate([q[:, i * HEAD_DIM:(i + 1) * HEAD_DIM] for i in range(g)], axis=0)
    qs = (qs * scale).astype(BF16)
    k = jnp.concatenate([kp_ref[...], kc_ref[...]], axis=0).astype(BF16)
    v = jnp.concatenate([vp_ref[...], vc_ref[...]], axis=0).astype(BF16)
    s = _dot_t(qs, k) + bias_ref[0]
    col = lax.broadcasted_iota(jnp.int32, s.shape, 1)
    s = jnp.where((pl.program_id(1) == 0) & (col < BLOCK), NEG, s)
    m = jnp.max(s, axis=-1, keepdims=True)
    if has_sink:
        m = jnp.maximum(m, sink_ref[0])
    p = jnp.exp(s - m)
    l = jnp.sum(p, axis=-1, keepdims=True)
    if has_sink:
        l = l + jnp.exp(sink_ref[0] - m)
    o = _dot(p.astype(BF16), v) / l
    for i in range(g):
        o_ref[:, i * HEAD_DIM:(i + 1) * HEAD_DIM] = o[i * BLOCK:(i + 1) * BLOCK]
    if want_lse:
        lse = jnp.broadcast_to(m + jnp.log(l), o.shape)
        for i in range(g):
            rest[1][:, i * HEAD_DIM:(i + 1) * HEAD_DIM] = lse[i * BLOCK:(i + 1) * BLOCK]


def _band_attention(src, q_col, k_col, v_col, n_seq, seq_len, n_kv, bias, sink, want_lse):
    g = N_HEADS // n_kv
    nb = seq_len // BLOCK
    qw = g * HEAD_DIM
    qc, kc, vc = q_col // qw, k_col // HEAD_DIM, v_col // HEAD_DIM
    cur = lambda c0: pl.BlockSpec((BLOCK, HEAD_DIM), lambda b, j, h: (b * nb + j, c0 + h))
    prev = lambda c0: pl.BlockSpec((BLOCK, HEAD_DIM), lambda b, j, h: (b * nb + jnp.maximum(j - 1, 0), c0 + h))
    in_specs = [pl.BlockSpec((BLOCK, qw), lambda b, j, h: (b * nb + j, qc + h)),
                prev(kc), cur(kc), prev(vc), cur(vc),
                pl.BlockSpec((1, g * BLOCK, 2 * BLOCK), lambda b, j, h: (h, 0, 0))]
    args = [src, src, src, src, src, bias]
    if sink is not None:
        in_specs.append(pl.BlockSpec((1, g * BLOCK, 1), lambda b, j, h: (h, 0, 0)))
        args.append(sink)
    out_spec = pl.BlockSpec((BLOCK, qw), lambda b, j, h: (b * nb + j, h))
    out_sds = jax.ShapeDtypeStruct((n_seq * seq_len, BRANCH), F32)
    res = pl.pallas_call(
        functools.partial(_band_kernel, g=g, has_sink=sink is not None, want_lse=want_lse, scale=HEAD_DIM ** -0.5),
        grid=(n_seq, nb, n_kv),
        in_specs=in_specs,
        out_specs=[out_spec, out_spec] if want_lse else out_spec,
        out_shape=[out_sds, out_sds] if want_lse else out_sds,
        compiler_params=_cparams("parallel", "parallel", "arbitrary"),
        name="band_attention",
    )(*args)
    return res


def _sink_rows(sink, n_kv, rows_per_head):
    g = N_HEADS // n_kv
    return jnp.broadcast_to(sink.astype(F32).reshape(n_kv, g, 1), (n_kv, g, rows_per_head)).reshape(n_kv, g * rows_per_head, 1)


def _window_decode_kernel(*refs, n_kv, n_past, n_new, has_sink, want_lse, scale):
    q_ref, buf_ref, new_ref, kn_ref, vn_ref, bb_ref, bn_ref = refs[:7]
    rest = refs[7:]
    if has_sink:
        sink_ref, rest = rest[0], rest[1:]
    o_ref, cache_ref = rest[0], rest[-1]
    rw = 2 * n_kv
    keep = (n_past - n_new) * rw
    cache_ref[pl.ds(0, keep), :] = buf_ref[pl.ds(n_new * rw, keep), :]
    cache_ref[pl.ds(keep, n_new * rw), :] = new_ref[...]
    for h in range(n_kv):
        kb = buf_ref[pl.ds(h, n_past, stride=rw), :].astype(BF16)
        vb = buf_ref[pl.ds(n_kv + h, n_past, stride=rw), :].astype(BF16)
        q = (q_ref[0, h] * scale).astype(BF16)
        s1 = _dot_t(q, kb) + bb_ref[h]
        s2 = _dot_t(q, kn_ref[0, h].astype(BF16)) + bn_ref[h]
        m = jnp.maximum(jnp.max(s1, axis=-1, keepdims=True), jnp.max(s2, axis=-1, keepdims=True))
        if has_sink:
            m = jnp.maximum(m, sink_ref[h])
        p1 = jnp.exp(s1 - m)
        p2 = jnp.exp(s2 - m)
        l = jnp.sum(p1, axis=-1, keepdims=True) + jnp.sum(p2, axis=-1, keepdims=True)
        if has_sink:
            l = l + jnp.exp(sink_ref[h] - m)
        o = (_dot(p1.astype(BF16), vb) + _dot(p2.astype(BF16), vn_ref[0, h].astype(BF16))) / l
        o_ref[0, h] = o
        if want_lse:
            rest[1][0, h] = jnp.broadcast_to(m + jnp.log(l), o.shape)


def _window_decode(q, buf_flat, new_flat, k_new, v_new, bias_buf, bias_new, sink, n_kv, n_past, n_new, want_lse):
    n_seq, _, rows, dh = q.shape
    rw = 2 * n_kv
    n_pad = k_new.shape[2]
    full = lambda shape: pl.BlockSpec(shape, lambda b: (0,) * len(shape))
    per_seq = lambda shape: pl.BlockSpec(shape, lambda b: (b,) + (0,) * (len(shape) - 1))
    in_specs = [per_seq((1, n_kv, rows, dh)), per_seq((n_past * rw, dh)), per_seq((n_new * rw, dh)),
                per_seq((1, n_kv, n_pad, dh)), per_seq((1, n_kv, n_pad, dh)),
                full((n_kv, rows, n_past)), full((n_kv, rows, n_pad))]
    args = [q, buf_flat, new_flat, k_new, v_new, bias_buf, bias_new]
    if sink is not None:
        in_specs.append(full((n_kv, rows, 1)))
        args.append(sink)
    o_spec, o_sds = per_seq((1, n_kv, rows, dh)), jax.ShapeDtypeStruct(q.shape, F32)
    out_specs = [o_spec] + ([o_spec] if want_lse else []) + [per_seq((n_past * rw, dh))]
    out_shape = [o_sds] + ([o_sds] if want_lse else []) + [jax.ShapeDtypeStruct(buf_flat.shape, F32)]
    return pl.pallas_call(
        functools.partial(_window_decode_kernel, n_kv=n_kv, n_past=n_past, n_new=n_new,
                          has_sink=sink is not None, want_lse=want_lse, scale=dh ** -0.5),
        grid=(n_seq,),
        in_specs=in_specs,
        out_specs=out_specs,
        out_shape=out_shape,
        compiler_params=_cparams("parallel"),
        name="window_decode",
    )(*args)


def _mla_q_kernel(cq_ref, ckv_ref, kpe_ref, cos_ref, sin_ref, qn_ref, kvn_ref, wn_ref, wp_ref, wps_ref, wuk_ref,
                  ql_ref, qp_ref, lat_ref, kpo_ref, latb_ref, kpb_ref, *, scale):
    def rms(x, g):
        return x * lax.rsqrt(jnp.mean(x * x, axis=-1, keepdims=True) + EPS) * g

    cq = rms(cq_ref[...], qn_ref[...]).astype(BF16)
    lat = rms(ckv_ref[...], kvn_ref[...])
    lat_ref[...] = lat
    latb_ref[...] = lat.astype(BF16)
    cos, sin = cos_ref[...], sin_ref[...]
    kp2 = kpe_ref[...]
    kp = kp2 * cos + pltpu.roll(kp2, ROPE_DIM, 1) * sin
    kpo_ref[...] = kp[:, :ROPE_DIM]
    lane = lax.broadcasted_iota(jnp.int32, kp.shape, 1)
    kpb_ref[...] = jnp.where(lane < ROPE_DIM, kp, 0.0).astype(BF16)
    n_rep = wp_ref.shape[1] // cos.shape[1]
    qp = _dot(cq, wp_ref[...]) * jnp.tile(cos, (1, n_rep)) + _dot(cq, wps_ref[...]) * jnp.tile(sin, (1, n_rep))
    qp_ref[...] = (qp * scale).astype(BF16)
    qn = _dot(cq, wn_ref[...])
    for h in range(N_HEADS):
        qh = qn[:, h * NOPE_DIM:(h + 1) * NOPE_DIM].astype(BF16)
        ql_ref[:, h * KV_LORA:(h + 1) * KV_LORA] = (_dot(qh, wuk_ref[h]) * scale).astype(BF16)


def _mla_q(z, cos, sin, q_norm, kv_norm, w_nope, w_pe, w_pe_sw, w_ukt, pos_tiles):
    m = z.shape[0]
    tm = _pick(m, (256, 128, 64, 32, 16, 8))
    c0 = BRANCH // Q_LORA
    full = lambda a: pl.BlockSpec(a.shape, lambda i: (0,) * a.ndim)
    row = lambda w: pl.BlockSpec((tm, w), lambda i: (i, 0))
    pos = pl.BlockSpec((tm, 128), lambda i: (i % pos_tiles, 0))
    qn, kvn = q_norm.reshape(1, Q_LORA), kv_norm.reshape(1, KV_LORA)
    return pl.pallas_call(
        functools.partial(_mla_q_kernel, scale=(NOPE_DIM + ROPE_DIM) ** -0.5),
        grid=(m // tm,),
        in_specs=[pl.BlockSpec((tm, Q_LORA), lambda i: (i, c0)),
                  pl.BlockSpec((tm, KV_LORA), lambda i: (i, c0 + 1)),
                  pl.BlockSpec((tm, 128), lambda i: (i, (BRANCH + Q_LORA + KV_LORA) // 128)),
                  pos, pos, full(qn), full(kvn), full(w_nope), full(w_pe), full(w_pe_sw), full(w_ukt)],
        out_specs=[row(N_HEADS * KV_LORA), row(N_HEADS * 128), row(KV_LORA), row(ROPE_DIM), row(KV_LORA), row(128)],
        out_shape=[jax.ShapeDtypeStruct((m, N_HEADS * KV_LORA), BF16),
                   jax.ShapeDtypeStruct((m, N_HEADS * 128), BF16),
                   jax.ShapeDtypeStruct((m, KV_LORA), F32),
                   jax.ShapeDtypeStruct((m, ROPE_DIM), F32),
                   jax.ShapeDtypeStruct((m, KV_LORA), BF16),
                   jax.ShapeDtypeStruct((m, 128), BF16)],
        compiler_params=_cparams("parallel"),
        name="mla_q",
    )(z, z, z, cos, sin, qn, kvn, w_nope, w_pe, w_pe_sw, w_ukt)


def _mla_prompt_kernel(ql_ref, qp_ref, c_ref, kp_ref, o_ref, qs_ref, qps_ref, m_ref, l_ref, acc_ref, *, tq, tk):
    qi, ki = pl.program_id(1), pl.program_id(2)
    last = (qi * tq + tq - 1) // tk

    @pl.when(ki == 0)
    def _():
        for h in range(N_HEADS):
            qs_ref[h * tq:(h + 1) * tq, :] = ql_ref[:, h * KV_LORA:(h + 1) * KV_LORA]
            qps_ref[h * tq:(h + 1) * tq, :] = qp_ref[:, h * 128:(h + 1) * 128]
        m_ref[...] = jnp.full(m_ref.shape, NEG, F32)
        l_ref[...] = jnp.zeros(l_ref.shape, F32)
        acc_ref[...] = jnp.zeros(acc_ref.shape, F32)

    @pl.when(ki <= last)
    def _():
        c = c_ref[...]
        s = _dot_t(qs_ref[...], c) + _dot_t(qps_ref[...], kp_ref[...])
        qpos = qi * tq + (lax.broadcasted_iota(jnp.int32, s.shape, 0) & (tq - 1))
        kpos = ki * tk + lax.broadcasted_iota(jnp.int32, s.shape, 1)
        s = jnp.where(kpos <= qpos, s, NEG)
        m_old = m_ref[...]
        m_new = jnp.maximum(m_old, jnp.max(s, axis=-1, keepdims=True))
        alpha = jnp.exp(m_old - m_new)
        p = jnp.exp(s - m_new)
        l_ref[...] = alpha * l_ref[...] + jnp.sum(p, axis=-1, keepdims=True)
        acc_ref[...] = alpha * acc_ref[...] + _dot(p.astype(BF16), c)
        m_ref[...] = m_new

    @pl.when(ki == last)
    def _():
        o = acc_ref[...] / l_ref[...]
        for h in range(N_HEADS):
            o_ref[:, h * KV_LORA:(h + 1) * KV_LORA] = o[h * tq:(h + 1) * tq].astype(BF16)


def _mla_prompt(q_lat, q_pe, lat_b, kpe_b, n_seq, seq_len):
    tq = BLOCK
    tk = _pick(seq_len, (512, 256, 128))
    nq, nk = seq_len // tq, seq_len // tk
    kidx = lambda b, i, j: (b * nk + jnp.minimum(j, (i * tq + tq - 1) // tk), 0)
    return pl.pallas_call(
        functools.partial(_mla_prompt_kernel, tq=tq, tk=tk),
        grid=(n_seq, nq, nk),
        in_specs=[pl.BlockSpec((tq, N_HEADS * KV_LORA), lambda b, i, j: (b * nq + i, 0)),
                  pl.BlockSpec((tq, N_HEADS * 128), lambda b, i, j: (b * nq + i, 0)),
                  pl.BlockSpec((tk, KV_LORA), kidx),
                  pl.BlockSpec((tk, 128), kidx)],
        out_specs=pl.BlockSpec((tq, N_HEADS * KV_LORA), lambda b, i, j: (b * nq + i, 0)),
        out_shape=jax.ShapeDtypeStruct((n_seq * seq_len, N_HEADS * KV_LORA), BF16),
        scratch_shapes=[pltpu.VMEM((N_HEADS * tq, KV_LORA), BF16), pltpu.VMEM((N_HEADS * tq, 128), BF16),
                        pltpu.VMEM((N_HEADS * tq, 1), F32), pltpu.VMEM((N_HEADS * tq, 1), F32),
                        pltpu.VMEM((N_HEADS * tq, KV_LORA), F32)],
        compiler_params=_cparams("parallel", "parallel", "arbitrary"),
        name="mla_prompt",
    )(q_lat, q_pe, lat_b, kpe_b)


def _mla_sample_kernel(pt_ref, ql_ref, qp_ref, *refs, pps):
    lat_refs, kpt_refs = refs[:pps], refs[pps:2 * pps]
    cn_ref, kn_ref, mask_ref, o_ref, m_ref, l_ref, acc_ref = refs[2 * pps:]
    p_idx = pl.program_id(1)

    @pl.when(p_idx == 0)
    def _():
        m_ref[...] = jnp.full(m_ref.shape, NEG, F32)
        l_ref[...] = jnp.zeros(l_ref.shape, F32)
        acc_ref[...] = jnp.zeros(acc_ref.shape, F32)

    ql, qp = ql_ref[0], qp_ref[0]

    def update(s, v):
        m_old = m_ref[...]
        m_new = jnp.maximum(m_old, jnp.max(s, axis=-1, keepdims=True))
        alpha = jnp.exp(m_old - m_new)
        p = jnp.exp(s - m_new)
        l_ref[...] = alpha * l_ref[...] + jnp.sum(p, axis=-1, keepdims=True)
        acc_ref[...] = alpha * acc_ref[...] + _dot(p.astype(BF16), v)
        m_ref[...] = m_new

    c = jnp.concatenate([r[0] for r in lat_refs], axis=0).astype(BF16)
    kpt = jnp.concatenate([r[0] for r in kpt_refs], axis=1).astype(BF16)
    update(_dot_t(ql, c) + _dot(qp, kpt), c)

    @pl.when(p_idx == pl.num_programs(1) - 1)
    def _():
        cn = cn_ref[0].astype(BF16)
        update(_dot_t(ql, cn) + _dot_t(qp, kn_ref[0].astype(BF16)) + mask_ref[...], cn)
        o_ref[0] = (acc_ref[...] / l_ref[...]).astype(BF16)


def _mla_sample(page_table, q_lat, q_pe, lat_pool, kpet_pool, c_new, k_new, mask_new):
    n_seq, rows, _ = q_lat.shape
    n_pages = page_table.shape[1]
    pps = _pick(n_pages, (PAGES_PER_STEP, 4, 2, 1))
    n_pad = c_new.shape[1]

    def page(i, shape):
        return pl.BlockSpec(shape, lambda b, p, pt: (pt[b, p * pps + i], 0, 0))

    per_seq = lambda shape: pl.BlockSpec(shape, lambda b, p, pt: (b,) + (0,) * (len(shape) - 1))
    grid_spec = pltpu.PrefetchScalarGridSpec(
        num_scalar_prefetch=1,
        grid=(n_seq, n_pages // pps),
        in_specs=[per_seq((1, rows, KV_LORA)), per_seq((1, rows, ROPE_DIM))]
        + [page(i, (1, PAGE, KV_LORA)) for i in range(pps)]
        + [page(i, (1, ROPE_DIM, PAGE)) for i in range(pps)]
        + [per_seq((1, n_pad, KV_LORA)), per_seq((1, n_pad, ROPE_DIM)),
           pl.BlockSpec((rows, n_pad), lambda b, p, pt: (0, 0))],
        out_specs=per_seq((1, rows, KV_LORA)),
        scratch_shapes=[pltpu.VMEM((rows, 1), F32), pltpu.VMEM((rows, 1), F32), pltpu.VMEM((rows, KV_LORA), F32)],
    )
    return pl.pallas_call(
        functools.partial(_mla_sample_kernel, pps=pps),
        grid_spec=grid_spec,
        out_shape=jax.ShapeDtypeStruct((n_seq, rows, KV_LORA), BF16),
        compiler_params=_cparams("parallel", "arbitrary"),
        name="mla_sample",
    )(page_table, q_lat, q_pe, *([lat_pool] * pps), *([kpet_pool] * pps), c_new, k_new, mask_new)


def _mla_o_kernel(ol_ref, w_ref, o_ref):
    for h in range(N_HEADS):
        o_ref[:, h * HEAD_DIM:(h + 1) * HEAD_DIM] = _dot(ol_ref[:, h * KV_LORA:(h + 1) * KV_LORA], w_ref[h])


def _mla_o(o_lat, w_uv):
    m = o_lat.shape[0]
    tm = _pick(m, (256, 128, 64, 32, 16, 8))
    return pl.pallas_call(
        _mla_o_kernel,
        grid=(m // tm,),
        in_specs=[pl.BlockSpec((tm, N_HEADS * KV_LORA), lambda i: (i, 0)),
                  pl.BlockSpec(w_uv.shape, lambda i: (0, 0, 0))],
        out_specs=pl.BlockSpec((tm, BRANCH), lambda i: (i, 0)),
        out_shape=jax.ShapeDtypeStruct((m, BRANCH), F32),
        compiler_params=_cparams("parallel"),
        name="mla_o",
    )(o_lat, w_uv)


def _diff_lambda(lam_ref):
    lp = lam_ref[...]
    lam_init = 0.8 - 0.6 * math.exp(-0.3 * D_LAYER)
    a = jnp.sum(lp[0:1] * lp[1:2], axis=-1, keepdims=True)
    b = jnp.sum(lp[2:3] * lp[3:4], axis=-1, keepdims=True)
    return jnp.exp(a) - jnp.exp(b) + lam_init, lam_init


def _diff_finish(acc, l, lam_ref, subln_ref):
    half = acc.shape[0] // 2
    lam, lam_init = _diff_lambda(lam_ref)
    o = acc / l
    o = o[:half] - lam * o[half:]
    y = o * lax.rsqrt(jnp.mean(o * o, axis=-1, keepdims=True) + EPS) * subln_ref[...]
    return y * (1.0 - lam_init)


def _split_components(q):
    lane = lax.broadcasted_iota(jnp.int32, q.shape, q.ndim - 1)
    return jnp.where(lane < DH_D, q, 0.0), jnp.where(lane < DH_D, 0.0, q)


def _diff_prompt_kernel(q_ref, k_ref, v_ref, bias_ref, lam_ref, subln_ref, o_ref, qs_ref, m_ref, l_ref, acc_ref, *, g, tq, scale):
    qi, ki = pl.program_id(2), pl.program_id(3)

    @pl.when(ki == 0)
    def _():
        q = q_ref[...] * scale
        ones, twos = [], []
        for i in range(g):
            a, b = _split_components(q[:, i * HEAD_DIM:(i + 1) * HEAD_DIM])
            ones.append(a)
            twos.append(b)
        qs_ref[...] = jnp.concatenate(ones + twos, axis=0).astype(BF16)
        m_ref[...] = jnp.full(m_ref.shape, NEG, F32)
        l_ref[...] = jnp.zeros(l_ref.shape, F32)
        acc_ref[...] = jnp.zeros(acc_ref.shape, F32)

    @pl.when(ki <= qi)
    def _():
        bias = bias_ref[0, 0]
        s = _dot_t(qs_ref[...], k_ref[...].astype(BF16)) + jnp.concatenate([bias, bias], axis=0)
        m_old = m_ref[...]
        m_new = jnp.maximum(m_old, jnp.max(s, axis=-1, keepdims=True))
        alpha = jnp.exp(m_old - m_new)
        p = jnp.exp(s - m_new)
        l_ref[...] = alpha * l_ref[...] + jnp.sum(p, axis=-1, keepdims=True)
        acc_ref[...] = alpha * acc_ref[...] + _dot(p.astype(BF16), v_ref[...].astype(BF16))
        m_ref[...] = m_new

    @pl.when(ki == qi)
    def _():
        y = _diff_finish(acc_ref[...], l_ref[...], lam_ref, subln_ref)
        for i in range(g):
            o_ref[:, i * HEAD_DIM:(i + 1) * HEAD_DIM] = y[i * tq:(i + 1) * tq]


def _diff_prompt(z, q_col, k_col, v_col, n_seq, seq_len, bias, lam_p, subln, tq):
    g = N_HEADS // KV_D
    nq = seq_len // tq
    qw = g * HEAD_DIM
    qc, kc, vc = q_col // qw, k_col // HEAD_DIM, v_col // HEAD_DIM
    kv_spec = lambda c0: pl.BlockSpec((tq, HEAD_DIM), lambda b, h, i, j: (b * nq + jnp.minimum(i, j), c0 + h))
    return pl.pallas_call(
        functools.partial(_diff_prompt_kernel, g=g, tq=tq, scale=DH_D ** -0.5),
        grid=(n_seq, KV_D, nq, nq),
        in_specs=[pl.BlockSpec((tq, qw), lambda b, h, i, j: (b * nq + i, qc + h)),
                  kv_spec(kc), kv_spec(vc),
                  pl.BlockSpec((1, 1, g * tq, tq), lambda b, h, i, j: (i - jnp.minimum(i, j), h, 0, 0)),
                  pl.BlockSpec(lam_p.shape, lambda b, h, i, j: (0, 0)),
                  pl.BlockSpec(subln.shape, lambda b, h, i, j: (0, 0))],
        out_specs=pl.BlockSpec((tq, qw), lambda b, h, i, j: (b * nq + i, h)),
        out_shape=jax.ShapeDtypeStruct((n_seq * seq_len, BRANCH), F32),
        scratch_shapes=[pltpu.VMEM((2 * g * tq, HEAD_DIM), BF16), pltpu.VMEM((2 * g * tq, 1), F32),
                        pltpu.VMEM((2 * g * tq, 1), F32), pltpu.VMEM((2 * g * tq, HEAD_DIM), F32)],
        compiler_params=_cparams("parallel", "parallel", "parallel", "arbitrary"),
        name="diff_prompt",
    )(z, z, z, bias, lam_p, subln)


def _diff_sample_kernel(pt_ref, q_ref, *refs, pps, scale):
    pages = refs[:pps]
    kn_ref, vn_ref, bp_ref, bn_ref, lam_ref, subln_ref, o_ref, m_ref, l_ref, acc_ref = refs[pps:]
    p_idx = pl.program_id(1)
    rw = 2 * KV_D

    @pl.when(p_idx == 0)
    def _():
        m_ref[...] = jnp.full(m_ref.shape, NEG, F32)
        l_ref[...] = jnp.zeros(l_ref.shape, F32)
        acc_ref[...] = jnp.zeros(acc_ref.shape, F32)

    def update(h, s, v):
        m_old = m_ref[h]
        m_new = jnp.maximum(m_old, jnp.max(s, axis=-1, keepdims=True))
        alpha = jnp.exp(m_old - m_new)
        p = jnp.exp(s - m_new)
        l_ref[h] = alpha * l_ref[h] + jnp.sum(p, axis=-1, keepdims=True)
        acc_ref[h] = alpha * acc_ref[h] + _dot(p.astype(BF16), v)
        m_ref[h] = m_new

    def queries(h):
        return (q_ref[0, h] * scale).astype(BF16)

    for h in range(KV_D):
        k = jnp.concatenate([r[pl.ds(h, PAGE, stride=rw), :] for r in pages], axis=0).astype(BF16)
        v = jnp.concatenate([r[pl.ds(KV_D + h, PAGE, stride=rw), :] for r in pages], axis=0).astype(BF16)
        bias = bp_ref[h]
        update(h, _dot_t(queries(h), k) + jnp.concatenate([bias, bias], axis=0), v)

    @pl.when(p_idx == pl.num_programs(1) - 1)
    def _():
        for h in range(KV_D):
            bias = bn_ref[h]
            s = _dot_t(queries(h), kn_ref[0, h].astype(BF16)) + jnp.concatenate([bias, bias], axis=0)
            update(h, s, vn_ref[0, h].astype(BF16))
            o_ref[0, h] = _diff_finish(acc_ref[h], l_ref[h], lam_ref, subln_ref)


def _diff_sample(page_table, q, pool_flat, k_new, v_new, bias_past, bias_new, lam_p, subln):
    n_seq, _, rows2, dh = q.shape
    rows = rows2 // 2
    n_pages = page_table.shape[1]
    pps = _pick(n_pages, (PAGES_PER_STEP, 4, 2, 1))
    n_pad = k_new.shape[2]
    rw = 2 * KV_D
    per_seq = lambda shape: pl.BlockSpec(shape, lambda b, p, pt: (b,) + (0,) * (len(shape) - 1))
    full = lambda shape: pl.BlockSpec(shape, lambda b, p, pt: (0,) * len(shape))
    grid_spec = pltpu.PrefetchScalarGridSpec(
        num_scalar_prefetch=1,
        grid=(n_seq, n_pages // pps),
        in_specs=[per_seq((1, KV_D, rows2, dh))]
        + [pl.BlockSpec((PAGE * rw, dh), functools.partial(lambda b, p, pt, i: (pt[b, p * pps + i], 0), i=i))
           for i in range(pps)]
        + [per_seq((1, KV_D, n_pad, dh)), per_seq((1, KV_D, n_pad, dh)),
           pl.BlockSpec((KV_D, rows, pps * PAGE), lambda b, p, pt: (0, 0, p)),
           full((KV_D, rows, n_pad)), full(lam_p.shape), full(subln.shape)],
        out_specs=per_seq((1, KV_D, rows, dh)),
        scratch_shapes=[pltpu.VMEM((KV_D, rows2, 1), F32), pltpu.VMEM((KV_D, rows2, 1), F32),
                        pltpu.VMEM((KV_D, rows2, dh), F32)],
    )
    return pl.pallas_call(
        functools.partial(_diff_sample_kernel, pps=pps, scale=DH_D ** -0.5),
        grid_spec=grid_spec,
        out_shape=jax.ShapeDtypeStruct((n_seq, KV_D, rows, dh), F32),
        compiler_params=_cparams("parallel", "arbitrary"),
        name="diff_sample",
    )(page_table, q, *([pool_flat] * pps), k_new, v_new, bias_past, bias_new, lam_p, subln)


def _heads_to_rows(x, n_seq, n_new, n_kv):
    g = N_HEADS // n_kv
    x = x.reshape(n_seq, n_new, n_kv, g, HEAD_DIM)
    return jnp.transpose(x, (0, 2, 3, 1, 4)).reshape(n_seq, n_kv, g * n_new, HEAD_DIM)


def _rows_to_heads(x, n_seq, n_new, n_kv):
    g = N_HEADS // n_kv
    x = x.reshape(n_seq, n_kv, g, n_new, HEAD_DIM)
    return jnp.transpose(x, (0, 3, 1, 2, 4)).reshape(n_seq * n_new, BRANCH)


def _new_kv(kv_cols, n_seq, n_new, n_kv, n_pad):
    kv = kv_cols.reshape(n_seq, n_new, 2, n_kv, HEAD_DIM)
    pad = lambda a: jnp.pad(jnp.transpose(a, (0, 2, 1, 3)), ((0, 0), (0, 0), (0, n_pad - n_new), (0, 0)))
    return kv_cols.reshape(n_seq * n_new * 2 * n_kv, HEAD_DIM), pad(kv[:, :, 0]), pad(kv[:, :, 1])


def _gate_first(w, gate_col):
    return jnp.concatenate([w[:, gate_col:gate_col + BRANCH], w[:, :gate_col], w[:, gate_col + BRANCH:]], axis=1)


def _swap_halves(w):
    half = w.shape[-1] // 2
    return jnp.concatenate([w[..., half:], w[..., :half]], axis=-1)


def _rope_tables(pos):
    half = ROPE_DIM // 2
    inv = ROPE_THETA ** (-jnp.arange(half, dtype=F32) / half)
    ang = pos.astype(F32)[:, None] * inv[None, :]
    cos, sin = jnp.cos(ang), jnp.sin(ang)
    return jnp.concatenate([cos, cos, cos, cos], axis=1), jnp.concatenate([-sin, sin, -sin, sin], axis=1)


def kernel(x_prompt, x_sample, cache_a_kv, cache_b_lat, cache_b_kpe, cache_c_kv1, cache_c_kv2, cache_c_kv3, cache_d_kv, page_table, rel_bias, ln_gain, final_gain, a_w_in, a_sink, a_w_out, b_w_in, b_q_norm, b_kv_norm, b_w_uq, b_w_uk, b_w_uv, b_w_out, c_w_in, c_w_out, d_w_in, d_lambda, d_subln, d_w_out):
    nb, t, d = x_prompt.shape
    ns, n_new, _ = x_sample.shape
    assert d == D_MODEL and ln_gain.shape[0] == 4 and t % (max(C_DILATIONS) * BLOCK) == 0
    n_past = page_table.shape[1] * PAGE
    mp, ms = nb * t, ns * n_new
    n_pad = 8
    hp, hs = x_prompt.reshape(mp, d), x_sample.reshape(ms, d)
    f = _t5_table(rel_bias, max(n_past + n_new, t) + n_pad)
    bf = lambda w: w.astype(BF16)

    nq, nk = BRANCH, KV_A * HEAD_DIM
    w_in = bf(_gate_first(a_w_in[0], nq + 2 * nk))
    w_out = bf(a_w_out[0])
    zp, zs = _rms_proj(hp, ln_gain[0], w_in), _rms_proj(hs, ln_gain[0], w_in)
    g_a = N_HEADS // KV_A
    o_p = _band_attention(zp, BRANCH, BRANCH + nq, BRANCH + nq + nk, nb, t, KV_A,
                          _band_bias(f, 1, KV_A), _sink_rows(a_sink[0], KV_A, BLOCK), False)
    a_kv_prompt = zp[:, BRANCH + nq:].reshape(nb, t, 2, KV_A, HEAD_DIM)[:, -min(WIN_A, t):][None]
    n_buf = cache_a_kv.shape[2]
    new_flat, k_new, v_new = _new_kv(zs[:, BRANCH + nq:], ns, n_new, KV_A, n_pad)
    bias_buf, bias_new = _decode_bias(f, n_buf, n_new, WIN_A, 1, KV_A, n_pad)
    o_s, cache = _window_decode(_heads_to_rows(zs[:, BRANCH:BRANCH + nq], ns, n_new, KV_A),
                                cache_a_kv[0].reshape(-1, HEAD_DIM), new_flat, k_new, v_new, bias_buf, bias_new,
                                _sink_rows(a_sink[0], KV_A, n_new), KV_A, n_buf, n_new, False)
    a_kv_sample = cache.reshape(cache_a_kv.shape)
    hp = _gated_out(hp, zp, [o_p], [], w_out)
    hs = _gated_out(hs, zs, [_rows_to_heads(o_s, ns, n_new, KV_A)], [], w_out)

    w = b_w_in[0]
    kpe_w = w[:, Q_LORA + KV_LORA:Q_LORA + KV_LORA + ROPE_DIM]
    w_in = bf(jnp.concatenate([w[:, Q_LORA + KV_LORA + ROPE_DIM:], w[:, :Q_LORA + KV_LORA], kpe_w, _swap_halves(kpe_w)], axis=1))
    w_out = bf(b_w_out[0])
    uq = b_w_uq[0]
    w_nope = bf(uq[:, :, :NOPE_DIM].reshape(Q_LORA, N_HEADS * NOPE_DIM))
    pad_pe = lambda a: jnp.pad(a, ((0, 0), (0, 0), (0, 128 - ROPE_DIM))).reshape(Q_LORA, N_HEADS * 128)
    w_pe, w_pe_sw = bf(pad_pe(uq[:, :, NOPE_DIM:])), bf(pad_pe(_swap_halves(uq[:, :, NOPE_DIM:])))
    w_ukt = bf(jnp.transpose(b_w_uk[0], (1, 2, 0)))
    w_uv = bf(jnp.transpose(b_w_uv[0], (1, 0, 2)))
    zp, zs = _rms_proj(hp, ln_gain[1], w_in), _rms_proj(hs, ln_gain[1], w_in)
    cos_p, sin_p = _rope_tables(jnp.arange(t))
    cos_s, sin_s = _rope_tables(n_past + jnp.arange(n_new))
    tmq = _pick(mp, (256, 128, 64, 32, 16, 8))
    assert t % tmq == 0
    ql_p, qp_p, lat_p, kpe_p, latb_p, kpeb_p = _mla_q(zp, cos_p, sin_p, b_q_norm[0], b_kv_norm[0],
                                                       w_nope, w_pe, w_pe_sw, w_ukt, t // tmq)
    ql_s, qp_s, lat_s, kpe_s, _, _ = _mla_q(zs, jnp.tile(cos_s, (ns, 1)), jnp.tile(sin_s, (ns, 1)), b_q_norm[0],
                                             b_kv_norm[0], w_nope, w_pe, w_pe_sw, w_ukt, ms)
    ol_p = _mla_prompt(ql_p, qp_p, latb_p, kpeb_p, nb, t)
    to_rows = lambda a, w_: jnp.transpose(a.reshape(ns, n_new, N_HEADS, w_), (0, 2, 1, 3)).reshape(ns, N_HEADS * n_new, w_)
    qls = to_rows(ql_s, KV_LORA)
    qps = to_rows(qp_s.reshape(ms, N_HEADS, 128)[:, :, :ROPE_DIM].reshape(ms, N_HEADS * ROPE_DIM), ROPE_DIM)
    padn = lambda a: jnp.pad(a.reshape(ns, n_new, -1), ((0, 0), (0, n_pad - n_new), (0, 0)))
    ii = jnp.arange(N_HEADS * n_new)[:, None] % n_new
    mask_new = jnp.where(jnp.arange(n_pad)[None, :] <= ii, 0.0, NEG).astype(F32)
    ol_s = _mla_sample(page_table, qls, qps, cache_b_lat[0], jnp.swapaxes(cache_b_kpe[0], -1, -2),
                       padn(lat_s), padn(kpe_s), mask_new)
    ol_s = jnp.transpose(ol_s.reshape(ns, N_HEADS, n_new, KV_LORA), (0, 2, 1, 3)).reshape(ms, N_HEADS * KV_LORA)
    hp = _gated_out(hp, zp, [_mla_o(ol_p, w_uv)], [], w_out)
    hs = _gated_out(hs, zs, [_mla_o(ol_s, w_uv)], [], w_out)
    b_lat_prompt, b_lat_sample = lat_p.reshape(1, nb, t, KV_LORA), lat_s.reshape(1, ns, n_new, KV_LORA)
    b_kpe_prompt, b_kpe_sample = kpe_p.reshape(1, nb, t, ROPE_DIM), kpe_s.reshape(1, ns, n_new, ROPE_DIM)

    n_grp = len(C_WINDOWS)
    gw = BRANCH + 2 * KV_C * HEAD_DIM
    w_in = bf(_gate_first(c_w_in[0], n_grp * gw))
    w_out = bf(c_w_out[0])
    zp, zs = _rms_proj(hp, ln_gain[2], w_in), _rms_proj(hs, ln_gain[2], w_in)
    outs_p, lses_p, outs_s, lses_s, c_prompt, c_sample = [], [], [], [], [], []
    for gi, (win, dil, buf) in enumerate(zip(C_WINDOWS, C_DILATIONS, (cache_c_kv1, cache_c_kv2, cache_c_kv3))):
        c0 = BRANCH + gi * gw
        kv_p = zp[:, c0 + BRANCH:c0 + gw]
        c_prompt.append(kv_p.reshape(nb, t, 2, KV_C, HEAD_DIM)[:, -min(win, t):][None])
        assert win // dil == BLOCK
        bias = _band_bias(f, dil, KV_C)
        if dil == 1:
            o, lse = _band_attention(zp, c0, c0 + BRANCH, c0 + BRANCH + KV_C * HEAD_DIM, nb, t, KV_C, bias, None, True)
        else:
            sub_len = t // dil
            sub = jnp.transpose(zp[:, c0:c0 + gw].reshape(nb, sub_len, dil, gw), (0, 2, 1, 3)).reshape(mp, gw)
            o, lse = _band_attention(sub, 0, BRANCH, BRANCH + KV_C * HEAD_DIM, nb * dil, sub_len, KV_C, bias, None, True)
            unsub = lambda a: jnp.transpose(a.reshape(nb, dil, sub_len, BRANCH), (0, 2, 1, 3)).reshape(mp, BRANCH)
            o, lse = unsub(o), unsub(lse)
        outs_p.append(o)
        lses_p.append(lse)
        n_buf = buf.shape[2]
        new_flat, k_new, v_new = _new_kv(zs[:, c0 + BRANCH:c0 + gw], ns, n_new, KV_C, n_pad)
        bias_buf, bias_new = _decode_bias(f, n_buf, n_new, win, dil, KV_C, n_pad)
        o, lse, cache = _window_decode(_heads_to_rows(zs[:, c0:c0 + BRANCH], ns, n_new, KV_C),
                                       buf[0].reshape(-1, HEAD_DIM), new_flat, k_new, v_new, bias_buf, bias_new,
                                       None, KV_C, n_buf, n_new, True)
        outs_s.append(_rows_to_heads(o, ns, n_new, KV_C))
        lses_s.append(_rows_to_heads(lse, ns, n_new, KV_C))
        c_sample.append(cache.reshape(buf.shape))
    hp = _gated_out(hp, zp, outs_p, lses_p, w_out)
    hs = _gated_out(hs, zs, outs_s, lses_s, w_out)

    nkv = 2 * KV_D * HEAD_DIM
    w_in = bf(_gate_first(d_w_in[0], BRANCH + nkv))
    w_out = bf(d_w_out[0])
    zp, zs = _rms_proj(hp, ln_gain[3], w_in), _rms_proj(hs, ln_gain[3], w_in)
    lam_p, subln = d_lambda[0].astype(F32), d_subln[0].reshape(1, HEAD_DIM).astype(F32)
    tq = _pick(t, (256, 128))
    o_p = _diff_prompt(zp, BRANCH, 2 * BRANCH, 2 * BRANCH + KV_D * HEAD_DIM, nb, t,
                       _causal_bias(f[:, :t], tq, KV_D), lam_p, subln, tq)
    d_kv_prompt = zp[:, 2 * BRANCH:].reshape(1, nb, t, 2, KV_D, HEAD_DIM)
    d_kv_sample = zs[:, 2 * BRANCH:].reshape(1, ns, n_new, 2, KV_D, HEAD_DIM)
    _, k_new, v_new = _new_kv(zs[:, 2 * BRANCH:], ns, n_new, KV_D, n_pad)
    bias_past, bias_new = _decode_bias(f, n_past, n_new, n_past + n_new, 1, KV_D, n_pad)
    q = _heads_to_rows(zs[:, BRANCH:2 * BRANCH], ns, n_new, KV_D)
    q = jnp.concatenate(_split_components(q), axis=2)
    o_s = _diff_sample(page_table, q, cache_d_kv[0].reshape(-1, HEAD_DIM), k_new, v_new, bias_past, bias_new, lam_p, subln)
    hp = _gated_out(hp, zp, [o_p], [], w_out)
    hs = _gated_out(hs, zs, [_rows_to_heads(o_s, ns, n_new, KV_D)], [], w_out)

    y_prompt = _rms_final(hp, final_gain).reshape(nb, t, d)
    y_sample = _rms_final(hs, final_gain).reshape(ns, n_new, d)
    return (y_prompt, y_sample, a_kv_prompt, a_kv_sample, b_lat_prompt, b_lat_sample, b_kpe_prompt, b_kpe_sample,
            c_prompt[0], c_sample[0], c_prompt[1], c_sample[1], c_prompt[2], c_sample[2], d_kv_prompt, d_kv_sample)
```

```python
import functools
import math

import jax
import jax.numpy as jnp
from jax import lax
from jax.experimental import pallas as pl
from jax.experimental.pallas import tpu as pltpu

F32 = jnp.float32
BF16 = jnp.bfloat16
NEG = -1e30
EPS = 1e-6

D_MODEL = 2048
HEAD_DIM = 128
N_HEADS = 16
BRANCH = N_HEADS * HEAD_DIM
PAGE = 128
BLOCK = 128
N_BUCKETS = 32
T5_MAX_DISTANCE = 2048
WIN_A, KV_A = 128, 2
Q_LORA, KV_LORA, NOPE_DIM, ROPE_DIM = 512, 512, 128, 64
ROPE_THETA = 10000.0
C_WINDOWS, C_DILATIONS, KV_C = (128, 512, 2048), (1, 4, 16), 4
DH_D, KV_D = 64, 4
D_LAYER = 3
PAGES_PER_STEP = 16
VMEM_LIMIT_BYTES = 56 * 1024 * 1024


def _cparams(*sem):
    return pltpu.CompilerParams(dimension_semantics=sem, vmem_limit_bytes=VMEM_LIMIT_BYTES)


def _pick(n, cands):
    for c in cands:
        if n % c == 0:
            return c
    raise ValueError(f"no tile for {n}")


_ROW_TILES = (512, 256, 128, 64, 32, 16, 8)


def _dot_t(a, b):
    return lax.dot_general(a, b, (((1,), (1,)), ((), ())), preferred_element_type=F32)


def _dot(a, b):
    return jnp.dot(a, b, preferred_element_type=F32)


def _rms_proj_kernel(x_ref, g_ref, w_ref, o_ref, xn_ref):
    @pl.when(pl.program_id(1) == 0)
    def _():
        x = x_ref[...]
        y = x * lax.rsqrt(jnp.mean(x * x, axis=-1, keepdims=True) + EPS) * g_ref[...]
        xn_ref[...] = y.astype(BF16)

    o_ref[...] = _dot(xn_ref[...], w_ref[...])


def _rms_proj(x, g, w):
    m, d = x.shape
    n = w.shape[1]
    tm = _pick(m, _ROW_TILES)
    tn = _pick(n, (1024, 768, 640, 512, 384, 256, 128))
    return pl.pallas_call(
        _rms_proj_kernel,
        grid=(m // tm, n // tn),
        in_specs=[pl.BlockSpec((tm, d), lambda i, j: (i, 0)),
                  pl.BlockSpec((1, d), lambda i, j: (0, 0)),
                  pl.BlockSpec((d, tn), lambda i, j: (0, j))],
        out_specs=pl.BlockSpec((tm, tn), lambda i, j: (i, j)),
        out_shape=jax.ShapeDtypeStruct((m, n), F32),
        scratch_shapes=[pltpu.VMEM((tm, d), BF16)],
        compiler_params=_cparams("parallel", "arbitrary"),
        name="rms_proj",
    )(x, g.reshape(1, d), w)


def _gated_out_kernel(*refs, n_groups):
    h_ref, gate_ref, w_ref = refs[:3]
    o_ref, a_ref = refs[-2], refs[-1]
    parts = refs[3:-2]

    @pl.when(pl.program_id(1) == 0)
    def _():
        if n_groups == 1:
            o = parts[0][...]
        else:
            outs = [r[...] for r in parts[:n_groups]]
            lses = [r[...] for r in parts[n_groups:]]
            m = functools.reduce(jnp.maximum, lses)
            es = [jnp.exp(l - m) for l in lses]
            den = functools.reduce(lambda a, b: a + b, es)
            o = functools.reduce(lambda a, b: a + b, [e * x for e, x in zip(es, outs)]) / den
        gate = gate_ref[...]
        a_ref[...] = (o * (gate * jax.nn.sigmoid(gate))).astype(BF16)

    o_ref[...] = h_ref[...] + _dot(a_ref[...], w_ref[...])


def _gated_out(h, z, outs, lses, w_out):
    m, d = h.shape
    n_groups = len(outs)
    tm = _pick(m, _ROW_TILES if n_groups == 1 else _ROW_TILES[1:])
    tn = 1024
    parts = list(outs) + (list(lses) if n_groups > 1 else [])
    row = pl.BlockSpec((tm, BRANCH), lambda i, j: (i, 0))
    return pl.pallas_call(
        functools.partial(_gated_out_kernel, n_groups=n_groups),
        grid=(m // tm, d // tn),
        in_specs=[pl.BlockSpec((tm, tn), lambda i, j: (i, j)), row,
                  pl.BlockSpec((BRANCH, tn), lambda i, j: (0, j))] + [row] * len(parts),
        out_specs=pl.BlockSpec((tm, tn), lambda i, j: (i, j)),
        out_shape=jax.ShapeDtypeStruct((m, d), F32),
        scratch_shapes=[pltpu.VMEM((tm, BRANCH), BF16)],
        compiler_params=_cparams("parallel", "arbitrary"),
        name="gated_out",
    )(h, z, w_out, *parts)


def _rms_kernel(x_ref, g_ref, o_ref):
    x = x_ref[...]
    o_ref[...] = x * lax.rsqrt(jnp.mean(x * x, axis=-1, keepdims=True) + EPS) * g_ref[...]


def _rms_final(x, g):
    m, d = x.shape
    tm = _pick(m, _ROW_TILES)
    return pl.pallas_call(
        _rms_kernel,
        grid=(m // tm,),
        in_specs=[pl.BlockSpec((tm, d), lambda i: (i, 0)), pl.BlockSpec((1, d), lambda i: (0, 0))],
        out_specs=pl.BlockSpec((tm, d), lambda i: (i, 0)),
        out_shape=jax.ShapeDtypeStruct((m, d), F32),
        compiler_params=_cparams("parallel"),
        name="rms_final",
    )(x, g.reshape(1, d))


def _t5_table(rel_bias, n_dist):
    exact = N_BUCKETS // 2
    dist = jnp.arange(n_dist)
    df = jnp.maximum(dist, 1).astype(F32)
    far = exact + (jnp.log(df / exact) / math.log(T5_MAX_DISTANCE / exact) * (N_BUCKETS - exact)).astype(jnp.int32)
    bucket = jnp.where(dist < exact, dist, jnp.minimum(far, N_BUCKETS - 1))
    return rel_bias[bucket].astype(F32).T


def _toeplitz(u, rows, cols):
    p = u.shape[-1]
    lead = u.shape[:-1]
    assert p >= rows + cols
    flat = jnp.tile(u, (1,) * len(lead) + (rows,))[..., :rows * (p - 1)]
    return flat.reshape(lead + (rows, p - 1))[..., :cols]


def _neg(*shape):
    return jnp.full(shape, NEG, F32)


def _band_bias(f, dil):
    val = f[:, :BLOCK * dil + 1:dil]
    u = jnp.concatenate([val[:, ::-1], _neg(N_HEADS, 3 * BLOCK - 1)], axis=1)
    return _toeplitz(u, BLOCK, 2 * BLOCK)


def _causal_bias_t(f, tq):
    n_delta = f.shape[1] // tq
    fp = jnp.concatenate([_neg(N_HEADS, tq), f, _neg(N_HEADS, tq)], axis=1)
    u = jnp.stack([jnp.concatenate([fp[:, tq + d * tq:2 * tq + d * tq], fp[:, d * tq:tq + d * tq]], axis=1)
                   for d in range(n_delta)])
    return _toeplitz(u, tq, tq)


def _decode_bias(f, n_past, n_new, window, dil, n_kv, n_pad):
    g = N_HEADS // n_kv
    n = n_past + n_new + 1
    dd = jnp.arange(n)
    val = jnp.where(((dd <= window) & (dd % dil == 0))[None, :], f[:, :n], NEG)
    rev = val[:, ::-1]
    u = jnp.concatenate([rev[:, n_new:], _neg(N_HEADS, n_pad), rev[:, :n_new]], axis=1)
    t = _toeplitz(u, n_new, n_past + n_pad).reshape(n_kv, g * n_new, n_past + n_pad)
    return t[..., :n_past], t[..., n_past:]


def _band_kernel(*refs, n_kv, has_sink, want_lse, scale):
    q_refs, (kp_ref, kc_ref, vp_ref, vc_ref, bias_ref) = refs[:2], refs[2:7]
    rest = refs[7:]
    if has_sink:
        sink_ref, rest = rest[0], rest[1:]
    o_ref = rest[0]
    g = N_HEADS // n_kv
    half = N_HEADS // 2

    def q_head(head):
        c = (head % half) * HEAD_DIM
        return q_refs[head // half][:, c:c + HEAD_DIM]

    first = pl.program_id(1) == 0
    for h in range(n_kv):
        cols = slice(h * HEAD_DIM, (h + 1) * HEAD_DIM)
        qs = jnp.concatenate([q_head(h * g + i) for i in range(g)], axis=0)
        qs = (qs * scale).astype(BF16)
        k = jnp.concatenate([kp_ref[:, cols], kc_ref[:, cols]], axis=0).astype(BF16)
        v = jnp.concatenate([vp_ref[:, cols], vc_ref[:, cols]], axis=0).astype(BF16)
        s = _dot_t(qs, k) + bias_ref[h]
        col = lax.broadcasted_iota(jnp.int32, s.shape, 1)
        s = jnp.where(first & (col < BLOCK), NEG, s)
        m = jnp.max(s, axis=-1, keepdims=True)
        if has_sink:
            m = jnp.maximum(m, sink_ref[h])
        p = jnp.exp(s - m)
        l = jnp.sum(p, axis=-1, keepdims=True)
        if has_sink:
            l = l + jnp.exp(sink_ref[h] - m)
        o = _dot(p.astype(BF16), v) / l
        for i in range(g):
            c = (h * g + i) * HEAD_DIM
            o_ref[:, c:c + HEAD_DIM] = o[i * BLOCK:(i + 1) * BLOCK]
        if want_lse:
            lse = jnp.broadcast_to(m + jnp.log(l), o.shape)
            for i in range(g):
                c = (h * g + i) * HEAD_DIM
                rest[1][:, c:c + HEAD_DIM] = lse[i * BLOCK:(i + 1) * BLOCK]


def _band_attention(src, q_col, k_col, v_col, n_seq, seq_len, n_kv, bias, sink, want_lse):
    g = N_HEADS // n_kv
    nb = seq_len // BLOCK
    qw, kw = BRANCH // 2, n_kv * HEAD_DIM
    qc, kc, vc = q_col // qw, k_col // kw, v_col // kw
    cur = lambda w, c0: pl.BlockSpec((BLOCK, w), lambda b, j: (b * nb + j, c0))
    prev = lambda w, c0: pl.BlockSpec((BLOCK, w), lambda b, j: (b * nb + jnp.maximum(j - 1, 0), c0))
    full = lambda a: pl.BlockSpec(a.shape, lambda b, j: (0,) * a.ndim)
    bias = bias.reshape(n_kv, g * BLOCK, 2 * BLOCK)
    in_specs = [cur(qw, qc), cur(qw, qc + 1), prev(kw, kc), cur(kw, kc), prev(kw, vc), cur(kw, vc), full(bias)]
    args = [src, src, src, src, src, src, bias]
    if sink is not None:
        in_specs.append(full(sink))
        args.append(sink)
    out_spec = pl.BlockSpec((BLOCK, BRANCH), lambda b, j: (b * nb + j, 0))
    out_sds = jax.ShapeDtypeStruct((n_seq * seq_len, BRANCH), F32)
    return pl.pallas_call(
        functools.partial(_band_kernel, n_kv=n_kv, has_sink=sink is not None, want_lse=want_lse, scale=HEAD_DIM ** -0.5),
        grid=(n_seq, nb),
        in_specs=in_specs,
        out_specs=[out_spec, out_spec] if want_lse else out_spec,
        out_shape=[out_sds, out_sds] if want_lse else out_sds,
        compiler_params=_cparams("parallel", "arbitrary"),
        name="band_attention",
    )(*args)


def _sink_rows(sink, n_kv, rows_per_head):
    g = N_HEADS // n_kv
    return jnp.broadcast_to(sink.astype(F32).reshape(n_kv, g, 1), (n_kv, g, rows_per_head)).reshape(n_kv, g * rows_per_head, 1)


def _window_decode_kernel(*refs, n_kv, n_past, n_new, has_sink, want_lse, scale):
    q_ref, buf_ref, new_ref, kn_ref, vn_ref, bb_ref, bn_ref = refs[:7]
    rest = refs[7:]
    if has_sink:
        sink_ref, rest = rest[0], rest[1:]
    o_ref, cache_ref = rest[0], rest[-1]
    rw = 2 * n_kv
    keep = (n_past - n_new) * rw
    cache_ref[pl.ds(0, keep), :] = buf_ref[pl.ds(n_new * rw, keep), :]
    cache_ref[pl.ds(keep, n_new * rw), :] = new_ref[...]
    for h in range(n_kv):
        kb = buf_ref[pl.ds(h, n_past, stride=rw), :].astype(BF16)
        vb = buf_ref[pl.ds(n_kv + h, n_past, stride=rw), :].astype(BF16)
        q = (q_ref[0, h] * scale).astype(BF16)
        s1 = _dot_t(q, kb) + bb_ref[h]
        s2 = _dot_t(q, kn_ref[0, h].astype(BF16)) + bn_ref[h]
        m = jnp.maximum(jnp.max(s1, axis=-1, keepdims=True), jnp.max(s2, axis=-1, keepdims=True))
        if has_sink:
            m = jnp.maximum(m, sink_ref[h])
        p1 = jnp.exp(s1 - m)
        p2 = jnp.exp(s2 - m)
        l = jnp.sum(p1, axis=-1, keepdims=True) + jnp.sum(p2, axis=-1, keepdims=True)
        if has_sink:
            l = l + jnp.exp(sink_ref[h] - m)
        o = (_dot(p1.astype(BF16), vb) + _dot(p2.astype(BF16), vn_ref[0, h].astype(BF16))) / l
        o_ref[0, h] = o
        if want_lse:
            rest[1][0, h] = jnp.broadcast_to(m + jnp.log(l), o.shape)


def _window_decode(q, buf_flat, new_flat, k_new, v_new, bias_buf, bias_new, sink, n_kv, n_past, n_new, want_lse):
    n_seq, _, rows, dh = q.shape
    rw = 2 * n_kv
    n_pad = k_new.shape[2]
    full = lambda shape: pl.BlockSpec(shape, lambda b: (0,) * len(shape))
    per_seq = lambda shape: pl.BlockSpec(shape, lambda b: (b,) + (0,) * (len(shape) - 1))
    in_specs = [per_seq((1, n_kv, rows, dh)), per_seq((n_past * rw, dh)), per_seq((n_new * rw, dh)),
                per_seq((1, n_kv, n_pad, dh)), per_seq((1, n_kv, n_pad, dh)),
                full((n_kv, rows, n_past)), full((n_kv, rows, n_pad))]
    args = [q, buf_flat, new_flat, k_new, v_new, bias_buf, bias_new]
    if sink is not None:
        in_specs.append(full((n_kv, rows, 1)))
        args.append(sink)
    o_spec, o_sds = per_seq((1, n_kv, rows, dh)), jax.ShapeDtypeStruct(q.shape, F32)
    out_specs = [o_spec] + ([o_spec] if want_lse else []) + [per_seq((n_past * rw, dh))]
    out_shape = [o_sds] + ([o_sds] if want_lse else []) + [jax.ShapeDtypeStruct(buf_flat.shape, F32)]
    return pl.pallas_call(
        functools.partial(_window_decode_kernel, n_kv=n_kv, n_past=n_past, n_new=n_new,
                          has_sink=sink is not None, want_lse=want_lse, scale=dh ** -0.5),
        grid=(n_seq,),
        in_specs=in_specs,
        out_specs=out_specs,
        out_shape=out_shape,
        compiler_params=_cparams("parallel"),
        name="window_decode",
    )(*args)


def _mla_q_kernel(cq_ref, ckv_ref, kpe_ref, cos_ref, sin_ref, qn_ref, kvn_ref, wn_ref, wp_ref, wps_ref, wuk_ref,
                  ql_ref, qp_ref, lat_ref, kpo_ref, latb_ref, kpb_ref, latt_ref, *, scale):
    def rms(x, g):
        return x * lax.rsqrt(jnp.mean(x * x, axis=-1, keepdims=True) + EPS) * g

    cq = rms(cq_ref[...], qn_ref[...]).astype(BF16)
    lat = rms(ckv_ref[...], kvn_ref[...])
    lat_ref[...] = lat
    latb_ref[...] = lat.astype(BF16)
    latt_ref[...] = lat.T.astype(BF16)
    cos, sin = cos_ref[...], sin_ref[...]
    kp2 = kpe_ref[...]
    kp = kp2 * cos + pltpu.roll(kp2, ROPE_DIM, 1) * sin
    kpo_ref[...] = kp[:, :ROPE_DIM]
    lane = lax.broadcasted_iota(jnp.int32, kp.shape, 1)
    kpb_ref[...] = jnp.where(lane < ROPE_DIM, kp, 0.0).astype(BF16)
    n_rep = wp_ref.shape[1] // cos.shape[1]
    qp = _dot(cq, wp_ref[...]) * jnp.tile(cos, (1, n_rep)) + _dot(cq, wps_ref[...]) * jnp.tile(sin, (1, n_rep))
    qp_ref[...] = (qp * scale).astype(BF16)
    qn = _dot(cq, wn_ref[...])
    for h in range(N_HEADS):
        qh = qn[:, h * NOPE_DIM:(h + 1) * NOPE_DIM].astype(BF16)
        ql_ref[:, h * KV_LORA:(h + 1) * KV_LORA] = (_dot(qh, wuk_ref[h]) * scale).astype(BF16)


def _mla_q(z, cos, sin, q_norm, kv_norm, w_nope, w_pe, w_pe_sw, w_ukt, pos_tiles):
    m = z.shape[0]
    tm = _pick(m, (256, 128, 64, 32, 16, 8))
    c0 = BRANCH // Q_LORA
    full = lambda a: pl.BlockSpec(a.shape, lambda i: (0,) * a.ndim)
    row = lambda w: pl.BlockSpec((tm, w), lambda i: (i, 0))
    pos = pl.BlockSpec((tm, 128), lambda i: (i % pos_tiles, 0))
    qn, kvn = q_norm.reshape(1, Q_LORA), kv_norm.reshape(1, KV_LORA)
    return pl.pallas_call(
        functools.partial(_mla_q_kernel, scale=(NOPE_DIM + ROPE_DIM) ** -0.5),
        grid=(m // tm,),
        in_specs=[pl.BlockSpec((tm, Q_LORA), lambda i: (i, c0)),
                  pl.BlockSpec((tm, KV_LORA), lambda i: (i, c0 + 1)),
                  pl.BlockSpec((tm, 128), lambda i: (i, (BRANCH + Q_LORA + KV_LORA) // 128)),
                  pos, pos, full(qn), full(kvn), full(w_nope), full(w_pe), full(w_pe_sw), full(w_ukt)],
        out_specs=[row(N_HEADS * KV_LORA), row(N_HEADS * 128), row(KV_LORA), row(ROPE_DIM), row(KV_LORA), row(128),
                   pl.BlockSpec((KV_LORA, tm), lambda i: (0, i))],
        out_shape=[jax.ShapeDtypeStruct((m, N_HEADS * KV_LORA), BF16),
                   jax.ShapeDtypeStruct((m, N_HEADS * 128), BF16),
                   jax.ShapeDtypeStruct((m, KV_LORA), F32),
                   jax.ShapeDtypeStruct((m, ROPE_DIM), F32),
                   jax.ShapeDtypeStruct((m, KV_LORA), BF16),
                   jax.ShapeDtypeStruct((m, 128), BF16),
                   jax.ShapeDtypeStruct((KV_LORA, m), BF16)],
        compiler_params=_cparams("parallel"),
        name="mla_q",
    )(z, z, z, cos, sin, qn, kvn, w_nope, w_pe, w_pe_sw, w_ukt)


def _mla_prompt_kernel(qi_ref, ki_ref, ql_ref, qp_ref, c_ref, ct_ref, kp_ref, wuv_ref, o_ref,
                       qs_ref, qps_ref, m_ref, l_ref, acc_ref, *, tq, tk):
    qi, ki = qi_ref[pl.program_id(1)], ki_ref[pl.program_id(1)]
    last = (qi * tq + tq - 1) // tk

    @pl.when(ki == 0)
    def _():
        for h in range(N_HEADS):
            qs_ref[h * tq:(h + 1) * tq, :] = ql_ref[:, h * KV_LORA:(h + 1) * KV_LORA]
            qps_ref[h * tq:(h + 1) * tq, :] = qp_ref[:, h * 128:(h + 1) * 128]
        m_ref[...] = jnp.full(m_ref.shape, NEG, F32)
        l_ref[...] = jnp.zeros(l_ref.shape, F32)
        acc_ref[...] = jnp.zeros(acc_ref.shape, F32)

    def step(masked):
        st = _dot_t(c_ref[...], qs_ref[...]) + _dot_t(kp_ref[...], qps_ref[...])
        if masked:
            kpos = ki * tk + lax.broadcasted_iota(jnp.int32, st.shape, 0)
            qpos = qi * tq + (lax.broadcasted_iota(jnp.int32, st.shape, 1) & (tq - 1))
            st = jnp.where(kpos <= qpos, st, NEG)
        m_old = m_ref[...]
        m_new = jnp.maximum(m_old, jnp.max(st, axis=0, keepdims=True))
        alpha = jnp.exp(m_old - m_new)
        p = jnp.exp(st - m_new)
        l_ref[...] = alpha * l_ref[...] + jnp.sum(p, axis=0, keepdims=True)
        acc_ref[...] = alpha * acc_ref[...] + _dot(ct_ref[...], p.astype(BF16))
        m_ref[...] = m_new

    @pl.when(ki < last)
    def _():
        step(False)

    @pl.when(ki == last)
    def _():
        step(True)
        ot = acc_ref[...] / l_ref[...]
        for h in range(N_HEADS):
            oh = ot[:, h * tq:(h + 1) * tq].T.astype(BF16)
            o_ref[:, h * HEAD_DIM:(h + 1) * HEAD_DIM] = _dot(oh, wuv_ref[h])


def _mla_prompt(q_lat, q_pe, lat_b, lat_t, kpe_b, w_uv, n_seq, seq_len):
    tq = BLOCK
    tk = _pick(seq_len, (512, 256, 128))
    nq, nk = seq_len // tq, seq_len // tk
    pairs = [(i, j) for i in range(nq) for j in range((i * tq + tq - 1) // tk + 1)]
    qi_tab = jnp.asarray([p[0] for p in pairs], jnp.int32)
    ki_tab = jnp.asarray([p[1] for p in pairs], jnp.int32)
    rows = N_HEADS * tq
    qrow = lambda w: pl.BlockSpec((tq, w), lambda b, p, qt, kt: (b * nq + qt[p], 0))
    krow = lambda w: pl.BlockSpec((tk, w), lambda b, p, qt, kt: (b * nk + kt[p], 0))
    grid_spec = pltpu.PrefetchScalarGridSpec(
        num_scalar_prefetch=2,
        grid=(n_seq, len(pairs)),
        in_specs=[qrow(N_HEADS * KV_LORA), qrow(N_HEADS * 128), krow(KV_LORA),
                  pl.BlockSpec((KV_LORA, tk), lambda b, p, qt, kt: (0, b * nk + kt[p])),
                  krow(128),
                  pl.BlockSpec(w_uv.shape, lambda b, p, qt, kt: (0, 0, 0))],
        out_specs=qrow(BRANCH),
        scratch_shapes=[pltpu.VMEM((rows, KV_LORA), BF16), pltpu.VMEM((rows, 128), BF16),
                        pltpu.VMEM((1, rows), F32), pltpu.VMEM((1, rows), F32),
                        pltpu.VMEM((KV_LORA, rows), F32)],
    )
    return pl.pallas_call(
        functools.partial(_mla_prompt_kernel, tq=tq, tk=tk),
        grid_spec=grid_spec,
        out_shape=jax.ShapeDtypeStruct((n_seq * seq_len, BRANCH), F32),
        compiler_params=_cparams("parallel", "arbitrary"),
        name="mla_prompt",
    )(qi_tab, ki_tab, q_lat, q_pe, lat_b, lat_t, kpe_b, w_uv)


def _mla_sample_kernel(pt_ref, ql_ref, qp_ref, *refs, pps):
    lat_refs, kpt_refs = refs[:pps], refs[pps:2 * pps]
    cn_ref, kn_ref, mask_ref, o_ref, m_ref, l_ref, acc_ref = refs[2 * pps:]
    p_idx = pl.program_id(1)

    @pl.when(p_idx == 0)
    def _():
        m_ref[...] = jnp.full(m_ref.shape, NEG, F32)
        l_ref[...] = jnp.zeros(l_ref.shape, F32)
        acc_ref[...] = jnp.zeros(acc_ref.shape, F32)

    ql, qp = ql_ref[0], qp_ref[0]

    def update(s, v):
        m_old = m_ref[...]
        m_new = jnp.maximum(m_old, jnp.max(s, axis=-1, keepdims=True))
        alpha = jnp.exp(m_old - m_new)
        p = jnp.exp(s - m_new)
        l_ref[...] = alpha * l_ref[...] + jnp.sum(p, axis=-1, keepdims=True)
        acc_ref[...] = alpha * acc_ref[...] + _dot(p.astype(BF16), v)
        m_ref[...] = m_new

    c = jnp.concatenate([r[0] for r in lat_refs], axis=0).astype(BF16)
    kpt = jnp.concatenate([r[0] for r in kpt_refs], axis=1).astype(BF16)
    update(_dot_t(ql, c) + _dot(qp, kpt), c)

    @pl.when(p_idx == pl.num_programs(1) - 1)
    def _():
        cn = cn_ref[0].astype(BF16)
        update(_dot_t(ql, cn) + _dot_t(qp, kn_ref[0].astype(BF16)) + mask_ref[...], cn)
        o_ref[0] = (acc_ref[...] / l_ref[...]).astype(BF16)


def _mla_sample(page_table, q_lat, q_pe, lat_pool, kpet_pool, c_new, k_new, mask_new):
    n_seq, rows, _ = q_lat.shape
    n_pages = page_table.shape[1]
    pps = _pick(n_pages, (PAGES_PER_STEP, 4, 2, 1))
    n_pad = c_new.shape[1]

    def page(i, shape):
        return pl.BlockSpec(shape, lambda b, p, pt: (pt[b, p * pps + i], 0, 0))

    per_seq = lambda shape: pl.BlockSpec(shape, lambda b, p, pt: (b,) + (0,) * (len(shape) - 1))
    grid_spec = pltpu.PrefetchScalarGridSpec(
        num_scalar_prefetch=1,
        grid=(n_seq, n_pages // pps),
        in_specs=[per_seq((1, rows, KV_LORA)), per_seq((1, rows, ROPE_DIM))]
        + [page(i, (1, PAGE, KV_LORA)) for i in range(pps)]
        + [page(i, (1, ROPE_DIM, PAGE)) for i in range(pps)]
        + [per_seq((1, n_pad, KV_LORA)), per_seq((1, n_pad, ROPE_DIM)),
           pl.BlockSpec((rows, n_pad), lambda b, p, pt: (0, 0))],
        out_specs=per_seq((1, rows, KV_LORA)),
        scratch_shapes=[pltpu.VMEM((rows, 1), F32), pltpu.VMEM((rows, 1), F32), pltpu.VMEM((rows, KV_LORA), F32)],
    )
    return pl.pallas_call(
        functools.partial(_mla_sample_kernel, pps=pps),
        grid_spec=grid_spec,
        out_shape=jax.ShapeDtypeStruct((n_seq, rows, KV_LORA), BF16),
        compiler_params=_cparams("parallel", "arbitrary"),
        name="mla_sample",
    )(page_table, q_lat, q_pe, *([lat_pool] * pps), *([kpet_pool] * pps), c_new, k_new, mask_new)


def _mla_o_kernel(ol_ref, w_ref, o_ref):
    for h in range(N_HEADS):
        o_ref[:, h * HEAD_DIM:(h + 1) * HEAD_DIM] = _dot(ol_ref[:, h * KV_LORA:(h + 1) * KV_LORA], w_ref[h])


def _mla_o(o_lat, w_uv):
    m = o_lat.shape[0]
    tm = _pick(m, (256, 128, 64, 32, 16, 8))
    return pl.pallas_call(
        _mla_o_kernel,
        grid=(m // tm,),
        in_specs=[pl.BlockSpec((tm, N_HEADS * KV_LORA), lambda i: (i, 0)),
                  pl.BlockSpec(w_uv.shape, lambda i: (0, 0, 0))],
        out_specs=pl.BlockSpec((tm, BRANCH), lambda i: (i, 0)),
        out_shape=jax.ShapeDtypeStruct((m, BRANCH), F32),
        compiler_params=_cparams("parallel"),
        name="mla_o",
    )(o_lat, w_uv)


def _diff_lambda(lam_ref):
    lp = lam_ref[...]
    lam_init = 0.8 - 0.6 * math.exp(-0.3 * D_LAYER)
    a = jnp.sum(lp[0:1] * lp[1:2], axis=-1, keepdims=True)
    b = jnp.sum(lp[2:3] * lp[3:4], axis=-1, keepdims=True)
    return jnp.exp(a) - jnp.exp(b) + lam_init, lam_init


def _diff_finish(acc, l, lam_ref, subln_ref):
    half = acc.shape[0] // 2
    lam, lam_init = _diff_lambda(lam_ref)
    o = acc / l
    o = o[:half] - lam * o[half:]
    y = o * lax.rsqrt(jnp.mean(o * o, axis=-1, keepdims=True) + EPS) * subln_ref[...]
    return y * (1.0 - lam_init)


def _split_components(q):
    lane = lax.broadcasted_iota(jnp.int32, q.shape, q.ndim - 1)
    return jnp.where(lane < DH_D, q, 0.0), jnp.where(lane < DH_D, 0.0, q)


def _diff_prompt_kernel(qi_ref, ki_ref, q_ref, k_ref, v_ref, bias_ref, lam_ref, subln_ref, o_ref,
                        qs_ref, m_ref, l_ref, acc_ref, *, g, tq, scale):
    qi, ki = qi_ref[pl.program_id(2)], ki_ref[pl.program_id(2)]

    @pl.when(ki == 0)
    def _():
        q = q_ref[...] * scale
        ones, twos = [], []
        for i in range(g):
            a, b = _split_components(q[:, i * HEAD_DIM:(i + 1) * HEAD_DIM])
            ones.append(a)
            twos.append(b)
        qs_ref[...] = jnp.concatenate(ones + twos, axis=0).astype(BF16)
        m_ref[...] = jnp.full(m_ref.shape, NEG, F32)
        l_ref[...] = jnp.zeros(l_ref.shape, F32)
        acc_ref[...] = jnp.zeros(acc_ref.shape, F32)

    st = _dot_t(k_ref[...].astype(BF16), qs_ref[...])
    st = st + jnp.concatenate([bias_ref[0, 0, i] for i in range(g)] * 2, axis=1)
    m_old = m_ref[...]
    m_new = jnp.maximum(m_old, jnp.max(st, axis=0, keepdims=True))
    alpha = jnp.exp(m_old - m_new)
    p = jnp.exp(st - m_new)
    l_ref[...] = alpha * l_ref[...] + jnp.sum(p, axis=0, keepdims=True)
    vt = v_ref[...].T.astype(BF16)
    acc_ref[...] = alpha * acc_ref[...] + _dot(vt, p.astype(BF16))
    m_ref[...] = m_new

    @pl.when(ki == qi)
    def _():
        ot = acc_ref[...] / l_ref[...]
        half = ot.shape[1] // 2
        lam, lam_init = _diff_lambda(lam_ref)
        d = ot[:, :half] - lam * ot[:, half:]
        y = d * lax.rsqrt(jnp.mean(d * d, axis=0, keepdims=True) + EPS) * subln_ref[...] * (1.0 - lam_init)
        yt = y.T
        for i in range(g):
            o_ref[:, i * HEAD_DIM:(i + 1) * HEAD_DIM] = yt[i * tq:(i + 1) * tq]


def _diff_prompt(z, q_col, k_col, v_col, n_seq, seq_len, bias, lam_p, subln, tq):
    g = N_HEADS // KV_D
    nq = seq_len // tq
    qw = g * HEAD_DIM
    qc, kc, vc = q_col // qw, k_col // HEAD_DIM, v_col // HEAD_DIM
    pairs = [(i, j) for i in range(nq) for j in range(i + 1)]
    qi_tab = jnp.asarray([p[0] for p in pairs], jnp.int32)
    ki_tab = jnp.asarray([p[1] for p in pairs], jnp.int32)
    rows = 2 * g * tq
    kv_spec = lambda c0: pl.BlockSpec((tq, HEAD_DIM), lambda b, h, p, qt, kt: (b * nq + kt[p], c0 + h))
    full = lambda a: pl.BlockSpec(a.shape, lambda b, h, p, qt, kt: (0,) * a.ndim)
    subln_col = subln.reshape(HEAD_DIM, 1)
    grid_spec = pltpu.PrefetchScalarGridSpec(
        num_scalar_prefetch=2,
        grid=(n_seq, KV_D, len(pairs)),
        in_specs=[pl.BlockSpec((tq, qw), lambda b, h, p, qt, kt: (b * nq + qt[p], qc + h)),
                  kv_spec(kc), kv_spec(vc),
                  pl.BlockSpec((1, 1, g, tq, tq), lambda b, h, p, qt, kt: (qt[p] - kt[p], h, 0, 0, 0)),
                  full(lam_p), full(subln_col)],
        out_specs=pl.BlockSpec((tq, qw), lambda b, h, p, qt, kt: (b * nq + qt[p], h)),
        scratch_shapes=[pltpu.VMEM((rows, HEAD_DIM), BF16), pltpu.VMEM((1, rows), F32),
                        pltpu.VMEM((1, rows), F32), pltpu.VMEM((HEAD_DIM, rows), F32)],
    )
    return pl.pallas_call(
        functools.partial(_diff_prompt_kernel, g=g, tq=tq, scale=DH_D ** -0.5),
        grid_spec=grid_spec,
        out_shape=jax.ShapeDtypeStruct((n_seq * seq_len, BRANCH), F32),
        compiler_params=_cparams("parallel", "parallel", "arbitrary"),
        name="diff_prompt",
    )(qi_tab, ki_tab, z, z, z, bias, lam_p, subln_col)


def _diff_sample_kernel(pt_ref, q_ref, *refs, pps, scale):
    pages = refs[:pps]
    kn_ref, vn_ref, bp_ref, bn_ref, lam_ref, subln_ref, o_ref, m_ref, l_ref, acc_ref = refs[pps:]
    p_idx = pl.program_id(1)
    rw = 2 * KV_D

    @pl.when(p_idx == 0)
    def _():
        m_ref[...] = jnp.full(m_ref.shape, NEG, F32)
        l_ref[...] = jnp.zeros(l_ref.shape, F32)
        acc_ref[...] = jnp.zeros(acc_ref.shape, F32)

    q = (q_ref[0] * scale).astype(BF16)

    def update(k, v, bias):
        s = jnp.einsum("hrd,hkd->hrk", q, k, preferred_element_type=F32) + jnp.concatenate([bias, bias], axis=1)
        m_old = m_ref[...]
        m_new = jnp.maximum(m_old, jnp.max(s, axis=-1, keepdims=True))
        alpha = jnp.exp(m_old - m_new)
        p = jnp.exp(s - m_new)
        l_ref[...] = alpha * l_ref[...] + jnp.sum(p, axis=-1, keepdims=True)
        pv = jnp.einsum("hrk,hkd->hrd", p.astype(BF16), v, preferred_element_type=F32)
        acc_ref[...] = alpha * acc_ref[...] + pv
        m_ref[...] = m_new

    def head_rows(first_row):
        return jnp.stack([jnp.concatenate([r[pl.ds(first_row + h, PAGE, stride=rw), :] for r in pages], axis=0)
                          for h in range(KV_D)]).astype(BF16)

    update(head_rows(0), head_rows(KV_D), bp_ref[...])

    @pl.when(p_idx == pl.num_programs(1) - 1)
    def _():
        update(kn_ref[0].astype(BF16), vn_ref[0].astype(BF16), bn_ref[...])
        for h in range(KV_D):
            o_ref[0, h] = _diff_finish(acc_ref[h], l_ref[h], lam_ref, subln_ref)


def _diff_sample(page_table, q, pool_flat, k_new, v_new, bias_past, bias_new, lam_p, subln):
    n_seq, _, rows2, dh = q.shape
    rows = rows2 // 2
    n_pages = page_table.shape[1]
    pps = _pick(n_pages, (PAGES_PER_STEP, 4, 2, 1))
    n_pad = k_new.shape[2]
    rw = 2 * KV_D
    per_seq = lambda shape: pl.BlockSpec(shape, lambda b, p, pt: (b,) + (0,) * (len(shape) - 1))
    full = lambda shape: pl.BlockSpec(shape, lambda b, p, pt: (0,) * len(shape))
    grid_spec = pltpu.PrefetchScalarGridSpec(
        num_scalar_prefetch=1,
        grid=(n_seq, n_pages // pps),
        in_specs=[per_seq((1, KV_D, rows2, dh))]
        + [pl.BlockSpec((PAGE * rw, dh), functools.partial(lambda b, p, pt, i: (pt[b, p * pps + i], 0), i=i))
           for i in range(pps)]
        + [per_seq((1, KV_D, n_pad, dh)), per_seq((1, KV_D, n_pad, dh)),
           pl.BlockSpec((KV_D, rows, pps * PAGE), lambda b, p, pt: (0, 0, p)),
           full((KV_D, rows, n_pad)), full(lam_p.shape), full(subln.shape)],
        out_specs=per_seq((1, KV_D, rows, dh)),
        scratch_shapes=[pltpu.VMEM((KV_D, rows2, 1), F32), pltpu.VMEM((KV_D, rows2, 1), F32),
                        pltpu.VMEM((KV_D, rows2, dh), F32)],
    )
    return pl.pallas_call(
        functools.partial(_diff_sample_kernel, pps=pps, scale=DH_D ** -0.5),
        grid_spec=grid_spec,
        out_shape=jax.ShapeDtypeStruct((n_seq, KV_D, rows, dh), F32),
        compiler_params=_cparams("parallel", "arbitrary"),
        name="diff_sample",
    )(page_table, q, *([pool_flat] * pps), k_new, v_new, bias_past, bias_new, lam_p, subln)


def _heads_to_rows(x, n_seq, n_new, n_kv):
    g = N_HEADS // n_kv
    x = x.reshape(n_seq, n_new, n_kv, g, HEAD_DIM)
    return jnp.transpose(x, (0, 2, 3, 1, 4)).reshape(n_seq, n_kv, g * n_new, HEAD_DIM)


def _rows_to_heads(x, n_seq, n_new, n_kv):
    g = N_HEADS // n_kv
    x = x.reshape(n_seq, n_kv, g, n_new, HEAD_DIM)
    return jnp.transpose(x, (0, 3, 1, 2, 4)).reshape(n_seq * n_new, BRANCH)


def _new_kv(kv_cols, n_seq, n_new, n_kv, n_pad):
    kv = kv_cols.reshape(n_seq, n_new, 2, n_kv, HEAD_DIM)
    pad = lambda a: jnp.pad(jnp.transpose(a, (0, 2, 1, 3)), ((0, 0), (0, 0), (0, n_pad - n_new), (0, 0)))
    return kv_cols.reshape(n_seq * n_new * 2 * n_kv, HEAD_DIM), pad(kv[:, :, 0]), pad(kv[:, :, 1])


def _gate_first(w, gate_col):
    return jnp.concatenate([w[:, gate_col:gate_col + BRANCH], w[:, :gate_col], w[:, gate_col + BRANCH:]], axis=1)


def _swap_halves(w):
    half = w.shape[-1] // 2
    return jnp.concatenate([w[..., half:], w[..., :half]], axis=-1)


def _rope_tables(pos):
    half = ROPE_DIM // 2
    inv = ROPE_THETA ** (-jnp.arange(half, dtype=F32) / half)
    ang = pos.astype(F32)[:, None] * inv[None, :]
    cos, sin = jnp.cos(ang), jnp.sin(ang)
    return jnp.concatenate([cos, cos, cos, cos], axis=1), jnp.concatenate([-sin, sin, -sin, sin], axis=1)


def kernel(x_prompt, x_sample, cache_a_kv, cache_b_lat, cache_b_kpe, cache_c_kv1, cache_c_kv2, cache_c_kv3, cache_d_kv, page_table, rel_bias, ln_gain, final_gain, a_w_in, a_sink, a_w_out, b_w_in, b_q_norm, b_kv_norm, b_w_uq, b_w_uk, b_w_uv, b_w_out, c_w_in, c_w_out, d_w_in, d_lambda, d_subln, d_w_out):
    nb, t, d = x_prompt.shape
    ns, n_new, _ = x_sample.shape
    assert d == D_MODEL and ln_gain.shape[0] == 4 and t % (max(C_DILATIONS) * BLOCK) == 0
    n_past = page_table.shape[1] * PAGE
    mp, ms = nb * t, ns * n_new
    n_pad = 8
    hp, hs = x_prompt.reshape(mp, d), x_sample.reshape(ms, d)
    f = _t5_table(rel_bias, max(n_past + n_new, t) + n_pad)
    bf = lambda w: w.astype(BF16)

    nq, nk = BRANCH, KV_A * HEAD_DIM
    w_in = bf(_gate_first(a_w_in[0], nq + 2 * nk))
    w_out = bf(a_w_out[0])
    zp, zs = _rms_proj(hp, ln_gain[0], w_in), _rms_proj(hs, ln_gain[0], w_in)
    band_bias = {dil: _band_bias(f, dil) for dil in C_DILATIONS}
    o_p = _band_attention(zp, BRANCH, BRANCH + nq, BRANCH + nq + nk, nb, t, KV_A,
                          band_bias[1], _sink_rows(a_sink[0], KV_A, BLOCK), False)
    a_kv_prompt = zp[:, BRANCH + nq:].reshape(nb, t, 2, KV_A, HEAD_DIM)[:, -min(WIN_A, t):][None]
    n_buf = cache_a_kv.shape[2]
    new_flat, k_new, v_new = _new_kv(zs[:, BRANCH + nq:], ns, n_new, KV_A, n_pad)
    bias_buf, bias_new = _decode_bias(f, n_buf, n_new, WIN_A, 1, KV_A, n_pad)
    o_s, cache = _window_decode(_heads_to_rows(zs[:, BRANCH:BRANCH + nq], ns, n_new, KV_A),
                                cache_a_kv[0].reshape(-1, HEAD_DIM), new_flat, k_new, v_new, bias_buf, bias_new,
                                _sink_rows(a_sink[0], KV_A, n_new), KV_A, n_buf, n_new, False)
    a_kv_sample = cache.reshape(cache_a_kv.shape)
    hp = _gated_out(hp, zp, [o_p], [], w_out)
    hs = _gated_out(hs, zs, [_rows_to_heads(o_s, ns, n_new, KV_A)], [], w_out)

    w = b_w_in[0]
    kpe_w = w[:, Q_LORA + KV_LORA:Q_LORA + KV_LORA + ROPE_DIM]
    w_in = bf(jnp.concatenate([w[:, Q_LORA + KV_LORA + ROPE_DIM:], w[:, :Q_LORA + KV_LORA], kpe_w, _swap_halves(kpe_w)], axis=1))
    w_out = bf(b_w_out[0])
    uq = b_w_uq[0]
    w_nope = bf(uq[:, :, :NOPE_DIM].reshape(Q_LORA, N_HEADS * NOPE_DIM))
    pad_pe = lambda a: jnp.pad(a, ((0, 0), (0, 0), (0, 128 - ROPE_DIM))).reshape(Q_LORA, N_HEADS * 128)
    w_pe, w_pe_sw = bf(pad_pe(uq[:, :, NOPE_DIM:])), bf(pad_pe(_swap_halves(uq[:, :, NOPE_DIM:])))
    w_ukt = bf(jnp.transpose(b_w_uk[0], (1, 2, 0)))
    w_uv = bf(jnp.transpose(b_w_uv[0], (1, 0, 2)))
    zp, zs = _rms_proj(hp, ln_gain[1], w_in), _rms_proj(hs, ln_gain[1], w_in)
    cos_p, sin_p = _rope_tables(jnp.arange(t))
    cos_s, sin_s = _rope_tables(n_past + jnp.arange(n_new))
    tmq = _pick(mp, (256, 128, 64, 32, 16, 8))
    assert t % tmq == 0
    ql_p, qp_p, lat_p, kpe_p, latb_p, kpeb_p, latt_p = _mla_q(zp, cos_p, sin_p, b_q_norm[0], b_kv_norm[0],
                                                               w_nope, w_pe, w_pe_sw, w_ukt, t // tmq)
    ql_s, qp_s, lat_s, kpe_s, _, _, _ = _mla_q(zs, jnp.tile(cos_s, (ns, 1)), jnp.tile(sin_s, (ns, 1)), b_q_norm[0],
                                                b_kv_norm[0], w_nope, w_pe, w_pe_sw, w_ukt, ms)
    o_p = _mla_prompt(ql_p, qp_p, latb_p, latt_p, kpeb_p, w_uv, nb, t)
    to_rows = lambda a, w_: jnp.transpose(a.reshape(ns, n_new, N_HEADS, w_), (0, 2, 1, 3)).reshape(ns, N_HEADS * n_new, w_)
    qls = to_rows(ql_s, KV_LORA)
    qps = to_rows(qp_s.reshape(ms, N_HEADS, 128)[:, :, :ROPE_DIM].reshape(ms, N_HEADS * ROPE_DIM), ROPE_DIM)
    padn = lambda a: jnp.pad(a.reshape(ns, n_new, -1), ((0, 0), (0, n_pad - n_new), (0, 0)))
    ii = jnp.arange(N_HEADS * n_new)[:, None] % n_new
    mask_new = jnp.where(jnp.arange(n_pad)[None, :] <= ii, 0.0, NEG).astype(F32)
    ol_s = _mla_sample(page_table, qls, qps, cache_b_lat[0], jnp.swapaxes(cache_b_kpe[0], -1, -2),
                       padn(lat_s), padn(kpe_s), mask_new)
    ol_s = jnp.transpose(ol_s.reshape(ns, N_HEADS, n_new, KV_LORA), (0, 2, 1, 3)).reshape(ms, N_HEADS * KV_LORA)
    hp = _gated_out(hp, zp, [o_p], [], w_out)
    hs = _gated_out(hs, zs, [_mla_o(ol_s, w_uv)], [], w_out)
    b_lat_prompt, b_lat_sample = lat_p.reshape(1, nb, t, KV_LORA), lat_s.reshape(1, ns, n_new, KV_LORA)
    b_kpe_prompt, b_kpe_sample = kpe_p.reshape(1, nb, t, ROPE_DIM), kpe_s.reshape(1, ns, n_new, ROPE_DIM)

    n_grp = len(C_WINDOWS)
    gw = BRANCH + 2 * KV_C * HEAD_DIM
    w_in = bf(_gate_first(c_w_in[0], n_grp * gw))
    w_out = bf(c_w_out[0])
    zp, zs = _rms_proj(hp, ln_gain[2], w_in), _rms_proj(hs, ln_gain[2], w_in)
    outs_p, lses_p, outs_s, lses_s, c_prompt, c_sample = [], [], [], [], [], []
    for gi, (win, dil, buf) in enumerate(zip(C_WINDOWS, C_DILATIONS, (cache_c_kv1, cache_c_kv2, cache_c_kv3))):
        c0 = BRANCH + gi * gw
        kv_p = zp[:, c0 + BRANCH:c0 + gw]
        c_prompt.append(kv_p.reshape(nb, t, 2, KV_C, HEAD_DIM)[:, -min(win, t):][None])
        assert win // dil == BLOCK
        bias = band_bias[dil]
        if dil == 1:
            o, lse = _band_attention(zp, c0, c0 + BRANCH, c0 + BRANCH + KV_C * HEAD_DIM, nb, t, KV_C, bias, None, True)
        else:
            sub_len = t // dil
            sub = jnp.transpose(zp[:, c0:c0 + gw].reshape(nb, sub_len, dil, gw), (0, 2, 1, 3)).reshape(mp, gw)
            o, lse = _band_attention(sub, 0, BRANCH, BRANCH + KV_C * HEAD_DIM, nb * dil, sub_len, KV_C, bias, None, True)
            unsub = lambda a: jnp.transpose(a.reshape(nb, dil, sub_len, BRANCH), (0, 2, 1, 3)).reshape(mp, BRANCH)
            o, lse = unsub(o), unsub(lse)
        outs_p.append(o)
        lses_p.append(lse)
        n_buf = buf.shape[2]
        new_flat, k_new, v_new = _new_kv(zs[:, c0 + BRANCH:c0 + gw], ns, n_new, KV_C, n_pad)
        bias_buf, bias_new = _decode_bias(f, n_buf, n_new, win, dil, KV_C, n_pad)
        o, lse, cache = _window_decode(_heads_to_rows(zs[:, c0:c0 + BRANCH], ns, n_new, KV_C),
                                       buf[0].reshape(-1, HEAD_DIM), new_flat, k_new, v_new, bias_buf, bias_new,
                                       None, KV_C, n_buf, n_new, True)
        outs_s.append(_rows_to_heads(o, ns, n_new, KV_C))
        lses_s.append(_rows_to_heads(lse, ns, n_new, KV_C))
        c_sample.append(cache.reshape(buf.shape))
    hp = _gated_out(hp, zp, outs_p, lses_p, w_out)
    hs = _gated_out(hs, zs, outs_s, lses_s, w_out)

    nkv = 2 * KV_D * HEAD_DIM
    w_in = bf(_gate_first(d_w_in[0], BRANCH + nkv))
    w_out = bf(d_w_out[0])
    zp, zs = _rms_proj(hp, ln_gain[3], w_in), _rms_proj(hs, ln_gain[3], w_in)
    lam_p, subln = d_lambda[0].astype(F32), d_subln[0].reshape(1, HEAD_DIM).astype(F32)
    tq = _pick(t, (256, 128))
    o_p = _diff_prompt(zp, BRANCH, 2 * BRANCH, 2 * BRANCH + KV_D * HEAD_DIM, nb, t,
                       _causal_bias_t(f[:, :t], tq).reshape(t // tq, KV_D, N_HEADS // KV_D, tq, tq), lam_p, subln, tq)
    d_kv_prompt = zp[:, 2 * BRANCH:].reshape(1, nb, t, 2, KV_D, HEAD_DIM)
    d_kv_sample = zs[:, 2 * BRANCH:].reshape(1, ns, n_new, 2, KV_D, HEAD_DIM)
    _, k_new, v_new = _new_kv(zs[:, 2 * BRANCH:], ns, n_new, KV_D, n_pad)
    bias_past, bias_new = _decode_bias(f, n_past, n_new, n_past + n_new, 1, KV_D, n_pad)
    q = _heads_to_rows(zs[:, BRANCH:2 * BRANCH], ns, n_new, KV_D)
    q = jnp.concatenate(_split_components(q), axis=2)
    o_s = _diff_sample(page_table, q, cache_d_kv[0].reshape(-1, HEAD_DIM), k_new, v_new, bias_past, bias_new, lam_p, subln)
    hp = _gated_out(hp, zp, [o_p], [], w_out)
    hs = _gated_out(hs, zs, [_rows_to_heads(o_s, ns, n_new, KV_D)], [], w_out)

    y_prompt = _rms_final(hp, final_gain).reshape(nb, t, d)
    y_sample = _rms_final(hs, final_gain).reshape(ns, n_new, d)
    return (y_prompt, y_sample, a_kv_prompt, a_kv_sample, b_lat_prompt, b_lat_sample, b_kpe_prompt, b_kpe_sample,
            c_prompt[0], c_sample[0], c_prompt[1], c_sample[1], c_prompt[2], c_sample[2], d_kv_prompt, d_kv_sample)
```

```python
import functools
import math

import jax
import jax.numpy as jnp
from jax import lax
from jax.experimental import pallas as pl
from jax.experimental.pallas import tpu as pltpu

F32 = jnp.float32
BF16 = jnp.bfloat16
NEG = -1e30
EPS = 1e-6

D_MODEL = 2048
HEAD_DIM = 128
N_HEADS = 16
BRANCH = N_HEADS * HEAD_DIM
PAGE = 128
BLOCK = 128
N_BUCKETS = 32
T5_MAX_DISTANCE = 2048
WIN_A, KV_A = 128, 2
Q_LORA, KV_LORA, NOPE_DIM, ROPE_DIM = 512, 512, 128, 64
ROPE_THETA = 10000.0
C_WINDOWS, C_DILATIONS, KV_C = (128, 512, 2048), (1, 4, 16), 4
DH_D, KV_D = 64, 4
D_LAYER = 3
PAGES_PER_STEP = 16
VMEM_LIMIT_BYTES = 56 * 1024 * 1024


def _cparams(*sem):
    return pltpu.CompilerParams(dimension_semantics=sem, vmem_limit_bytes=VMEM_LIMIT_BYTES)


def _pick(n, cands):
    for c in cands:
        if n % c == 0:
            return c
    raise ValueError(f"no tile for {n}")


_ROW_TILES = (512, 256, 128, 64, 32, 16, 8)


def _dot_t(a, b):
    return lax.dot_general(a, b, (((1,), (1,)), ((), ())), preferred_element_type=F32)


def _dot(a, b):
    return jnp.dot(a, b, preferred_element_type=F32)


def _rms_proj_kernel(x_ref, g_ref, w_ref, o_ref, xn_ref):
    @pl.when(pl.program_id(1) == 0)
    def _():
        x = x_ref[...]
        y = x * lax.rsqrt(jnp.mean(x * x, axis=-1, keepdims=True) + EPS) * g_ref[...]
        xn_ref[...] = y.astype(BF16)

    o_ref[...] = _dot(xn_ref[...], w_ref[...])


def _rms_proj(x, g, w):
    m, d = x.shape
    n = w.shape[1]
    tm = _pick(m, (1024,) + _ROW_TILES)
    tn = _pick(n, (1024, 768, 640, 512, 384, 256, 128))
    return pl.pallas_call(
        _rms_proj_kernel,
        grid=(m // tm, n // tn),
        in_specs=[pl.BlockSpec((tm, d), lambda i, j: (i, 0)),
                  pl.BlockSpec((1, d), lambda i, j: (0, 0)),
                  pl.BlockSpec((d, tn), lambda i, j: (0, j))],
        out_specs=pl.BlockSpec((tm, tn), lambda i, j: (i, j)),
        out_shape=jax.ShapeDtypeStruct((m, n), F32),
        scratch_shapes=[pltpu.VMEM((tm, d), BF16)],
        compiler_params=_cparams("parallel", "arbitrary"),
        name="rms_proj",
    )(x, g.reshape(1, d), w)


def _gated_out_kernel(*refs, n_groups):
    h_ref, gate_ref, w_ref = refs[:3]
    o_ref = refs[-1]
    parts = refs[3:-1]
    if n_groups == 1:
        o = parts[0][...].astype(F32)
    else:
        outs = [r[...] for r in parts[:n_groups]]
        lses = [r[...] for r in parts[n_groups:]]
        m = functools.reduce(jnp.maximum, lses)
        es = [jnp.exp(l - m) for l in lses]
        den = functools.reduce(lambda a, b: a + b, es)
        o = functools.reduce(lambda a, b: a + b, [e * x for e, x in zip(es, outs)]) / den
    gate = gate_ref[...]
    a = (o * (gate * jax.nn.sigmoid(gate))).astype(BF16)
    o_ref[...] = h_ref[...] + _dot(a, w_ref[...])


def _gated_out(h, z, outs, lses, w_out):
    m, d = h.shape
    n_groups = len(outs)
    tm = _pick(m, _ROW_TILES if n_groups == 1 else _ROW_TILES[1:])
    parts = list(outs) + (list(lses) if n_groups > 1 else [])
    row = pl.BlockSpec((tm, BRANCH), lambda i: (i, 0))
    return pl.pallas_call(
        functools.partial(_gated_out_kernel, n_groups=n_groups),
        grid=(m // tm,),
        in_specs=[pl.BlockSpec((tm, d), lambda i: (i, 0)), row,
                  pl.BlockSpec((BRANCH, d), lambda i: (0, 0), pipeline_mode=pl.Buffered(1))] + [row] * len(parts),
        out_specs=pl.BlockSpec((tm, d), lambda i: (i, 0)),
        out_shape=jax.ShapeDtypeStruct((m, d), F32),
        compiler_params=_cparams("parallel"),
        name="gated_out",
    )(h, z, w_out, *parts)


def _rms_kernel(x_ref, g_ref, o_ref):
    x = x_ref[...]
    o_ref[...] = x * lax.rsqrt(jnp.mean(x * x, axis=-1, keepdims=True) + EPS) * g_ref[...]


def _rms_final(x, g):
    m, d = x.shape
    tm = _pick(m, _ROW_TILES)
    return pl.pallas_call(
        _rms_kernel,
        grid=(m // tm,),
        in_specs=[pl.BlockSpec((tm, d), lambda i: (i, 0)), pl.BlockSpec((1, d), lambda i: (0, 0))],
        out_specs=pl.BlockSpec((tm, d), lambda i: (i, 0)),
        out_shape=jax.ShapeDtypeStruct((m, d), F32),
        compiler_params=_cparams("parallel"),
        name="rms_final",
    )(x, g.reshape(1, d))


def _t5_table(rel_bias, n_dist):
    exact = N_BUCKETS // 2
    dist = jnp.arange(n_dist)
    df = jnp.maximum(dist, 1).astype(F32)
    far = exact + (jnp.log(df / exact) / math.log(T5_MAX_DISTANCE / exact) * (N_BUCKETS - exact)).astype(jnp.int32)
    bucket = jnp.where(dist < exact, dist, jnp.minimum(far, N_BUCKETS - 1))
    return rel_bias[bucket].astype(F32).T, rel_bias[bucket[::-1]].astype(F32).T


def _toeplitz(u, rows, cols):
    p = u.shape[-1]
    lead = u.shape[:-1]
    assert p >= rows + cols
    flat = jnp.tile(u, (1,) * len(lead) + (rows,))[..., :rows * (p - 1)]
    return flat.reshape(lead + (rows, p - 1))[..., :cols]


def _neg(*shape):
    return jnp.full(shape, NEG, F32)


def _band_bias(f_rev, dil):
    n = f_rev.shape[1]
    val_rev = f_rev[:, n - 1 - BLOCK * dil::dil][:, :BLOCK + 1]
    u = jnp.concatenate([val_rev, _neg(N_HEADS, 3 * BLOCK - 1)], axis=1)
    return _toeplitz(u, BLOCK, 2 * BLOCK)


def _causal_bias_t(f, tq):
    n_delta = f.shape[1] // tq
    fp = jnp.concatenate([_neg(N_HEADS, tq), f, _neg(N_HEADS, tq)], axis=1)
    u = jnp.stack([jnp.concatenate([fp[:, tq + d * tq:2 * tq + d * tq], fp[:, d * tq:tq + d * tq]], axis=1)
                   for d in range(n_delta)])
    return _toeplitz(u, tq, tq)


def _decode_bias(f_rev, n_past, n_new, window, dil, n_kv, n_pad):
    g = N_HEADS // n_kv
    n = n_past + n_new + 1
    dd = n - 1 - jnp.arange(n)
    rev = jnp.where(((dd <= window) & (dd % dil == 0))[None, :], f_rev[:, f_rev.shape[1] - n:], NEG)
    u = jnp.concatenate([rev[:, n_new:], _neg(N_HEADS, n_pad), rev[:, :n_new]], axis=1)
    t = _toeplitz(u, n_new, n_past + n_pad).reshape(n_kv, g * n_new, n_past + n_pad)
    return t[..., :n_past], t[..., n_past:]


def _band_kernel(*refs, n_kv, has_sink, want_lse, scale):
    q_refs, (kp_ref, kc_ref, vp_ref, vc_ref, bias_ref) = refs[:2], refs[2:7]
    rest = refs[7:]
    if has_sink:
        sink_ref, rest = rest[0], rest[1:]
    o_ref = rest[0]
    g = N_HEADS // n_kv
    half = N_HEADS // 2

    def q_head(head):
        c = (head % half) * HEAD_DIM
        return q_refs[head // half][:, c:c + HEAD_DIM]

    first = pl.program_id(1) == 0
    for h in range(n_kv):
        cols = slice(h * HEAD_DIM, (h + 1) * HEAD_DIM)
        qs = jnp.concatenate([q_head(h * g + i) for i in range(g)], axis=0)
        qs = (qs * scale).astype(BF16)
        k = jnp.concatenate([kp_ref[:, cols], kc_ref[:, cols]], axis=0).astype(BF16)
        v = jnp.concatenate([vp_ref[:, cols], vc_ref[:, cols]], axis=0).astype(BF16)
        s = _dot_t(qs, k) + bias_ref[h]
        col = lax.broadcasted_iota(jnp.int32, s.shape, 1)
        s = jnp.where(first & (col < BLOCK), NEG, s)
        m = jnp.max(s, axis=-1, keepdims=True)
        if has_sink:
            m = jnp.maximum(m, sink_ref[h])
        p = jnp.exp(s - m)
        l = jnp.sum(p, axis=-1, keepdims=True)
        if has_sink:
            l = l + jnp.exp(sink_ref[h] - m)
        o = _dot(p.astype(BF16), v) / l
        for i in range(g):
            c = (h * g + i) * HEAD_DIM
            o_ref[:, c:c + HEAD_DIM] = o[i * BLOCK:(i + 1) * BLOCK].astype(o_ref.dtype)
        if want_lse:
            lse = jnp.broadcast_to(m + jnp.log(l), o.shape)
            for i in range(g):
                c = (h * g + i) * HEAD_DIM
                rest[1][:, c:c + HEAD_DIM] = lse[i * BLOCK:(i + 1) * BLOCK]


def _band_attention(src, q_col, k_col, v_col, n_seq, seq_len, n_kv, bias, sink, want_lse):
    g = N_HEADS // n_kv
    nb = seq_len // BLOCK
    qw, kw = BRANCH // 2, n_kv * HEAD_DIM
    qc, kc, vc = q_col // qw, k_col // kw, v_col // kw
    cur = lambda w, c0: pl.BlockSpec((BLOCK, w), lambda b, j: (b * nb + j, c0))
    prev = lambda w, c0: pl.BlockSpec((BLOCK, w), lambda b, j: (b * nb + jnp.maximum(j - 1, 0), c0))
    full = lambda a: pl.BlockSpec(a.shape, lambda b, j: (0,) * a.ndim)
    bias = bias.reshape(n_kv, g * BLOCK, 2 * BLOCK)
    in_specs = [cur(qw, qc), cur(qw, qc + 1), prev(kw, kc), cur(kw, kc), prev(kw, vc), cur(kw, vc), full(bias)]
    args = [src, src, src, src, src, src, bias]
    if sink is not None:
        in_specs.append(full(sink))
        args.append(sink)
    out_spec = pl.BlockSpec((BLOCK, BRANCH), lambda b, j: (b * nb + j, 0))
    out_sds = jax.ShapeDtypeStruct((n_seq * seq_len, BRANCH), F32 if want_lse else BF16)
    return pl.pallas_call(
        functools.partial(_band_kernel, n_kv=n_kv, has_sink=sink is not None, want_lse=want_lse, scale=HEAD_DIM ** -0.5),
        grid=(n_seq, nb),
        in_specs=in_specs,
        out_specs=[out_spec, out_spec] if want_lse else out_spec,
        out_shape=[out_sds, out_sds] if want_lse else out_sds,
        compiler_params=_cparams("parallel", "arbitrary"),
        name="band_attention",
    )(*args)


def _dilated_band_kernel(q_ref, kp_ref, kc_ref, vp_ref, vc_ref, bias_ref, o_ref, lse_ref, *, dil, scale):
    first = pl.program_id(1) == 0
    bias = bias_ref[0]
    for ph in range(dil):
        rows = pl.ds(ph, BLOCK, stride=dil)
        q = (q_ref[rows, :] * scale).astype(BF16)
        k = jnp.concatenate([kp_ref[rows, :], kc_ref[rows, :]], axis=0).astype(BF16)
        v = jnp.concatenate([vp_ref[rows, :], vc_ref[rows, :]], axis=0).astype(BF16)
        s = _dot_t(q, k) + bias
        col = lax.broadcasted_iota(jnp.int32, s.shape, 1)
        s = jnp.where(first & (col < BLOCK), NEG, s)
        m = jnp.max(s, axis=-1, keepdims=True)
        p = jnp.exp(s - m)
        l = jnp.sum(p, axis=-1, keepdims=True)
        o = _dot(p.astype(BF16), v) / l
        o_ref[rows, :] = o
        lse_ref[rows, :] = jnp.broadcast_to(m + jnp.log(l), o.shape)


def _dilated_band_attention(src, q_col, k_col, v_col, n_seq, seq_len, n_kv, dil, bias):
    g = N_HEADS // n_kv
    span = dil * BLOCK
    n_span = seq_len // span
    qc, kc, vc = q_col // HEAD_DIM, k_col // HEAD_DIM, v_col // HEAD_DIM
    cur = lambda c0, per: pl.BlockSpec((span, HEAD_DIM), lambda b, j, h: (b * n_span + j, c0 + h // per))
    prev = lambda c0, per: pl.BlockSpec((span, HEAD_DIM),
                                        lambda b, j, h: (b * n_span + jnp.maximum(j - 1, 0), c0 + h // per))
    out_spec = cur(0, 1)
    out_sds = jax.ShapeDtypeStruct((n_seq * seq_len, BRANCH), F32)
    return pl.pallas_call(
        functools.partial(_dilated_band_kernel, dil=dil, scale=HEAD_DIM ** -0.5),
        grid=(n_seq, n_span, N_HEADS),
        in_specs=[cur(qc, 1), prev(kc, g), cur(kc, g), prev(vc, g), cur(vc, g),
                  pl.BlockSpec((1, BLOCK, 2 * BLOCK), lambda b, j, h: (h, 0, 0))],
        out_specs=[out_spec, out_spec],
        out_shape=[out_sds, out_sds],
        compiler_params=_cparams("parallel", "arbitrary", "arbitrary"),
        name="dilated_band_attention",
    )(src, src, src, src, src, bias)


def _sink_rows(sink, n_kv, rows_per_head):
    g = N_HEADS // n_kv
    return jnp.broadcast_to(sink.astype(F32).reshape(n_kv, g, 1), (n_kv, g, rows_per_head)).reshape(n_kv, g * rows_per_head, 1)


def _window_decode_kernel(*refs, n_kv, n_past, n_new, sps, has_sink, want_lse, scale):
    q_ref, buf_ref, new_ref, kn_ref, vn_ref, bb_ref, bn_ref = refs[:7]
    rest = refs[7:]
    if has_sink:
        sink_ref, rest = rest[0], rest[1:]
    o_ref, cache_ref = rest[0], rest[-1]
    rw = 2 * n_kv
    keep = (n_past - n_new) * rw
    for s in range(sps):
        base = s * n_past * rw
        cache_ref[pl.ds(base, keep), :] = buf_ref[pl.ds(base + n_new * rw, keep), :]
        cache_ref[pl.ds(base + keep, n_new * rw), :] = new_ref[pl.ds(s * n_new * rw, n_new * rw), :]
        for h in range(n_kv):
            kb = buf_ref[pl.ds(base + h, n_past, stride=rw), :].astype(BF16)
            vb = buf_ref[pl.ds(base + n_kv + h, n_past, stride=rw), :].astype(BF16)
            q = (q_ref[s, h] * scale).astype(BF16)
            s1 = _dot_t(q, kb) + bb_ref[h]
            s2 = _dot_t(q, kn_ref[s, h].astype(BF16)) + bn_ref[h]
            m = jnp.maximum(jnp.max(s1, axis=-1, keepdims=True), jnp.max(s2, axis=-1, keepdims=True))
            if has_sink:
                m = jnp.maximum(m, sink_ref[h])
            p1 = jnp.exp(s1 - m)
            p2 = jnp.exp(s2 - m)
            l = jnp.sum(p1, axis=-1, keepdims=True) + jnp.sum(p2, axis=-1, keepdims=True)
            if has_sink:
                l = l + jnp.exp(sink_ref[h] - m)
            o = (_dot(p1.astype(BF16), vb) + _dot(p2.astype(BF16), vn_ref[s, h].astype(BF16))) / l
            o_ref[s, h] = o
            if want_lse:
                rest[1][s, h] = jnp.broadcast_to(m + jnp.log(l), o.shape)


def _window_decode(q, buf_flat, new_flat, k_new, v_new, bias_buf, bias_new, sink, n_kv, n_past, n_new, want_lse):
    n_seq, _, rows, dh = q.shape
    rw = 2 * n_kv
    n_pad = k_new.shape[2]
    sps = _pick(n_seq, [c for c in (8, 4, 2, 1) if c * n_past <= 1024 or c == 1])
    full = lambda shape: pl.BlockSpec(shape, lambda b: (0,) * len(shape))
    per_seq = lambda shape: pl.BlockSpec(shape, lambda b: (b,) + (0,) * (len(shape) - 1))
    in_specs = [per_seq((sps, n_kv, rows, dh)), per_seq((sps * n_past * rw, dh)), per_seq((sps * n_new * rw, dh)),
                per_seq((sps, n_kv, n_pad, dh)), per_seq((sps, n_kv, n_pad, dh)),
                full((n_kv, rows, n_past)), full((n_kv, rows, n_pad))]
    args = [q, buf_flat, new_flat, k_new, v_new, bias_buf, bias_new]
    if sink is not None:
        in_specs.append(full((n_kv, rows, 1)))
        args.append(sink)
    o_spec, o_sds = per_seq((sps, n_kv, rows, dh)), jax.ShapeDtypeStruct(q.shape, F32)
    out_specs = [o_spec] + ([o_spec] if want_lse else []) + [per_seq((sps * n_past * rw, dh))]
    out_shape = [o_sds] + ([o_sds] if want_lse else []) + [jax.ShapeDtypeStruct(buf_flat.shape, F32)]
    return pl.pallas_call(
        functools.partial(_window_decode_kernel, n_kv=n_kv, n_past=n_past, n_new=n_new, sps=sps,
                          has_sink=sink is not None, want_lse=want_lse, scale=dh ** -0.5),
        grid=(n_seq // sps,),
        in_specs=in_specs,
        out_specs=out_specs,
        out_shape=out_shape,
        compiler_params=_cparams("parallel"),
        name="window_decode",
    )(*args)


def _mla_q_kernel(cq_ref, ckv_ref, kpe_ref, cos_ref, sin_ref, qn_ref, kvn_ref, wn_ref, wp_ref, wps_ref, wuk_ref,
                  ql_ref, qp_ref, lat_ref, kpo_ref, latb_ref, kpb_ref, latt_ref, *, scale):
    def rms(x, g):
        return x * lax.rsqrt(jnp.mean(x * x, axis=-1, keepdims=True) + EPS) * g

    cq = rms(cq_ref[...], qn_ref[...]).astype(BF16)
    lat = rms(ckv_ref[...], kvn_ref[...])
    lat_ref[...] = lat
    latb_ref[...] = lat.astype(BF16)
    latt_ref[...] = lat.T.astype(BF16)
    cos, sin = cos_ref[...], sin_ref[...]
    kp2 = kpe_ref[...]
    kp = kp2 * cos + pltpu.roll(kp2, ROPE_DIM, 1) * sin
    kpo_ref[...] = kp[:, :ROPE_DIM]
    lane = lax.broadcasted_iota(jnp.int32, kp.shape, 1)
    kpb_ref[...] = jnp.where(lane < ROPE_DIM, kp, 0.0).astype(BF16)
    n_rep = wp_ref.shape[1] // cos.shape[1]
    qp = _dot(cq, wp_ref[...]) * jnp.tile(cos, (1, n_rep)) + _dot(cq, wps_ref[...]) * jnp.tile(sin, (1, n_rep))
    qp_ref[...] = (qp * scale).astype(BF16)
    qn = _dot(cq, wn_ref[...])
    for h in range(N_HEADS):
        qh = qn[:, h * NOPE_DIM:(h + 1) * NOPE_DIM].astype(BF16)
        ql_ref[:, h * KV_LORA:(h + 1) * KV_LORA] = (_dot(qh, wuk_ref[h]) * scale).astype(BF16)


def _mla_q(z, cos, sin, q_norm, kv_norm, w_nope, w_pe, w_pe_sw, w_ukt, pos_tiles):
    m = z.shape[0]
    tm = _pick(m, (256, 128, 64, 32, 16, 8))
    c0 = BRANCH // Q_LORA
    full = lambda a: pl.BlockSpec(a.shape, lambda i: (0,) * a.ndim)
    row = lambda w: pl.BlockSpec((tm, w), lambda i: (i, 0))
    pos = pl.BlockSpec((tm, 128), lambda i: (i % pos_tiles, 0))
    qn, kvn = q_norm.reshape(1, Q_LORA), kv_norm.reshape(1, KV_LORA)
    return pl.pallas_call(
        functools.partial(_mla_q_kernel, scale=(NOPE_DIM + ROPE_DIM) ** -0.5),
        grid=(m // tm,),
        in_specs=[pl.BlockSpec((tm, Q_LORA), lambda i: (i, c0)),
                  pl.BlockSpec((tm, KV_LORA), lambda i: (i, c0 + 1)),
                  pl.BlockSpec((tm, 128), lambda i: (i, (BRANCH + Q_LORA + KV_LORA) // 128)),
                  pos, pos, full(qn), full(kvn), full(w_nope), full(w_pe), full(w_pe_sw), full(w_ukt)],
        out_specs=[row(N_HEADS * KV_LORA), row(N_HEADS * 128), row(KV_LORA), row(ROPE_DIM), row(KV_LORA), row(128),
                   pl.BlockSpec((KV_LORA, tm), lambda i: (0, i))],
        out_shape=[jax.ShapeDtypeStruct((m, N_HEADS * KV_LORA), BF16),
                   jax.ShapeDtypeStruct((m, N_HEADS * 128), BF16),
                   jax.ShapeDtypeStruct((m, KV_LORA), F32),
                   jax.ShapeDtypeStruct((m, ROPE_DIM), F32),
                   jax.ShapeDtypeStruct((m, KV_LORA), BF16),
                   jax.ShapeDtypeStruct((m, 128), BF16),
                   jax.ShapeDtypeStruct((KV_LORA, m), BF16)],
        compiler_params=_cparams("parallel"),
        name="mla_q",
    )(z, z, z, cos, sin, qn, kvn, w_nope, w_pe, w_pe_sw, w_ukt)


def _mla_prompt_kernel(qi_ref, ki_ref, ql_ref, qp_ref, c_ref, ct_ref, kp_ref, wuv_ref, o_ref,
                       qs_ref, qps_ref, m_ref, l_ref, acc_ref, *, tq, tk):
    qi, ki = qi_ref[pl.program_id(1)], ki_ref[pl.program_id(1)]
    last = (qi * tq + tq - 1) // tk

    @pl.when(ki == 0)
    def _():
        for h in range(N_HEADS):
            qs_ref[h * tq:(h + 1) * tq, :] = ql_ref[:, h * KV_LORA:(h + 1) * KV_LORA]
            qps_ref[h * tq:(h + 1) * tq, :] = qp_ref[:, h * 128:(h + 1) * 128]
        m_ref[...] = jnp.full(m_ref.shape, NEG, F32)
        l_ref[...] = jnp.zeros(l_ref.shape, F32)
        acc_ref[...] = jnp.zeros(acc_ref.shape, F32)

    def step(masked):
        st = _dot_t(c_ref[...], qs_ref[...]) + _dot_t(kp_ref[...], qps_ref[...])
        if masked:
            kpos = ki * tk + lax.broadcasted_iota(jnp.int32, st.shape, 0)
            qpos = qi * tq + (lax.broadcasted_iota(jnp.int32, st.shape, 1) & (tq - 1))
            st = jnp.where(kpos <= qpos, st, NEG)
        m_old = m_ref[...]
        m_new = jnp.maximum(m_old, jnp.max(st, axis=0, keepdims=True))
        alpha = jnp.exp(m_old - m_new)
        p = jnp.exp(st - m_new)
        l_ref[...] = alpha * l_ref[...] + jnp.sum(p, axis=0, keepdims=True)
        acc_ref[...] = alpha * acc_ref[...] + _dot(ct_ref[...], p.astype(BF16))
        m_ref[...] = m_new

    @pl.when(ki < last)
    def _():
        step(False)

    @pl.when(ki == last)
    def _():
        step(True)
        ot = acc_ref[...] / l_ref[...]
        for h in range(N_HEADS):
            oh = ot[:, h * tq:(h + 1) * tq].T.astype(BF16)
            o_ref[:, h * HEAD_DIM:(h + 1) * HEAD_DIM] = _dot(oh, wuv_ref[h]).astype(o_ref.dtype)


def _mla_prompt(q_lat, q_pe, lat_b, lat_t, kpe_b, w_uv, n_seq, seq_len):
    tq = BLOCK
    tk = _pick(seq_len, (512, 256, 128))
    nq, nk = seq_len // tq, seq_len // tk
    pairs = [(i, j) for i in range(nq) for j in range((i * tq + tq - 1) // tk + 1)]
    qi_tab = jnp.asarray([p[0] for p in pairs], jnp.int32)
    ki_tab = jnp.asarray([p[1] for p in pairs], jnp.int32)
    rows = N_HEADS * tq
    qrow = lambda w: pl.BlockSpec((tq, w), lambda b, p, qt, kt: (b * nq + qt[p], 0))
    krow = lambda w: pl.BlockSpec((tk, w), lambda b, p, qt, kt: (b * nk + kt[p], 0))
    grid_spec = pltpu.PrefetchScalarGridSpec(
        num_scalar_prefetch=2,
        grid=(n_seq, len(pairs)),
        in_specs=[qrow(N_HEADS * KV_LORA), qrow(N_HEADS * 128), krow(KV_LORA),
                  pl.BlockSpec((KV_LORA, tk), lambda b, p, qt, kt: (0, b * nk + kt[p])),
                  krow(128),
                  pl.BlockSpec(w_uv.shape, lambda b, p, qt, kt: (0, 0, 0))],
        out_specs=qrow(BRANCH),
        scratch_shapes=[pltpu.VMEM((rows, KV_LORA), BF16), pltpu.VMEM((rows, 128), BF16),
                        pltpu.VMEM((1, rows), F32), pltpu.VMEM((1, rows), F32),
                        pltpu.VMEM((KV_LORA, rows), F32)],
    )
    return pl.pallas_call(
        functools.partial(_mla_prompt_kernel, tq=tq, tk=tk),
        grid_spec=grid_spec,
        out_shape=jax.ShapeDtypeStruct((n_seq * seq_len, BRANCH), BF16),
        compiler_params=_cparams("parallel", "arbitrary"),
        name="mla_prompt",
    )(qi_tab, ki_tab, q_lat, q_pe, lat_b, lat_t, kpe_b, w_uv)


def _mla_sample_kernel(pt_ref, ql_ref, qp_ref, *refs, pps):
    lat_refs, kpt_refs = refs[:pps], refs[pps:2 * pps]
    cn_ref, kn_ref, mask_ref, o_ref, m_ref, l_ref, acc_ref = refs[2 * pps:]
    p_idx = pl.program_id(1)

    @pl.when(p_idx == 0)
    def _():
        m_ref[...] = jnp.full(m_ref.shape, NEG, F32)
        l_ref[...] = jnp.zeros(l_ref.shape, F32)
        acc_ref[...] = jnp.zeros(acc_ref.shape, F32)

    ql, qp = ql_ref[0], qp_ref[0]

    def update(s, v):
        m_old = m_ref[...]
        m_new = jnp.maximum(m_old, jnp.max(s, axis=-1, keepdims=True))
        alpha = jnp.exp(m_old - m_new)
        p = jnp.exp(s - m_new)
        l_ref[...] = alpha * l_ref[...] + jnp.sum(p, axis=-1, keepdims=True)
        acc_ref[...] = alpha * acc_ref[...] + _dot(p.astype(BF16), v)
        m_ref[...] = m_new

    c = jnp.concatenate([r[0] for r in lat_refs], axis=0).astype(BF16)
    kpt = jnp.concatenate([r[0] for r in kpt_refs], axis=1).astype(BF16)
    update(_dot_t(ql, c) + _dot(qp, kpt), c)

    @pl.when(p_idx == pl.num_programs(1) - 1)
    def _():
        cn = cn_ref[0].astype(BF16)
        update(_dot_t(ql, cn) + _dot_t(qp, kn_ref[0].astype(BF16)) + mask_ref[...], cn)
        o_ref[0] = (acc_ref[...] / l_ref[...]).astype(BF16)


def _mla_sample(page_table, q_lat, q_pe, lat_pool, kpet_pool, c_new, k_new, mask_new):
    n_seq, rows, _ = q_lat.shape
    n_pages = page_table.shape[1]
    pps = _pick(n_pages, (PAGES_PER_STEP, 4, 2, 1))
    n_pad = c_new.shape[1]

    def page(i, shape):
        return pl.BlockSpec(shape, lambda b, p, pt: (pt[b, p * pps + i], 0, 0))

    per_seq = lambda shape: pl.BlockSpec(shape, lambda b, p, pt: (b,) + (0,) * (len(shape) - 1))
    grid_spec = pltpu.PrefetchScalarGridSpec(
        num_scalar_prefetch=1,
        grid=(n_seq, n_pages // pps),
        in_specs=[per_seq((1, rows, KV_LORA)), per_seq((1, rows, ROPE_DIM))]
        + [page(i, (1, PAGE, KV_LORA)) for i in range(pps)]
        + [page(i, (1, ROPE_DIM, PAGE)) for i in range(pps)]
        + [per_seq((1, n_pad, KV_LORA)), per_seq((1, n_pad, ROPE_DIM)),
           pl.BlockSpec((rows, n_pad), lambda b, p, pt: (0, 0))],
        out_specs=per_seq((1, rows, KV_LORA)),
        scratch_shapes=[pltpu.VMEM((rows, 1), F32), pltpu.VMEM((rows, 1), F32), pltpu.VMEM((rows, KV_LORA), F32)],
    )
    return pl.pallas_call(
        functools.partial(_mla_sample_kernel, pps=pps),
        grid_spec=grid_spec,
        out_shape=jax.ShapeDtypeStruct((n_seq, rows, KV_LORA), BF16),
        compiler_params=_cparams("parallel", "arbitrary"),
        name="mla_sample",
    )(page_table, q_lat, q_pe, *([lat_pool] * pps), *([kpet_pool] * pps), c_new, k_new, mask_new)


def _mla_o_kernel(ol_ref, w_ref, o_ref):
    for h in range(N_HEADS):
        o_ref[:, h * HEAD_DIM:(h + 1) * HEAD_DIM] = _dot(ol_ref[:, h * KV_LORA:(h + 1) * KV_LORA], w_ref[h])


def _mla_o(o_lat, w_uv):
    m = o_lat.shape[0]
    tm = _pick(m, (256, 128, 64, 32, 16, 8))
    return pl.pallas_call(
        _mla_o_kernel,
        grid=(m // tm,),
        in_specs=[pl.BlockSpec((tm, N_HEADS * KV_LORA), lambda i: (i, 0)),
                  pl.BlockSpec(w_uv.shape, lambda i: (0, 0, 0))],
        out_specs=pl.BlockSpec((tm, BRANCH), lambda i: (i, 0)),
        out_shape=jax.ShapeDtypeStruct((m, BRANCH), F32),
        compiler_params=_cparams("parallel"),
        name="mla_o",
    )(o_lat, w_uv)


def _diff_lambda(lam_ref):
    lp = lam_ref[...]
    lam_init = 0.8 - 0.6 * math.exp(-0.3 * D_LAYER)
    a = jnp.sum(lp[0:1] * lp[1:2], axis=-1, keepdims=True)
    b = jnp.sum(lp[2:3] * lp[3:4], axis=-1, keepdims=True)
    return jnp.exp(a) - jnp.exp(b) + lam_init, lam_init


def _diff_finish(acc, l, lam_ref, subln_ref):
    half = acc.shape[0] // 2
    lam, lam_init = _diff_lambda(lam_ref)
    o = acc / l
    o = o[:half] - lam * o[half:]
    y = o * lax.rsqrt(jnp.mean(o * o, axis=-1, keepdims=True) + EPS) * subln_ref[...]
    return y * (1.0 - lam_init)


def _split_components(q):
    lane = lax.broadcasted_iota(jnp.int32, q.shape, q.ndim - 1)
    return jnp.where(lane < DH_D, q, 0.0), jnp.where(lane < DH_D, 0.0, q)


def _diff_prompt_kernel(qi_ref, ki_ref, q_ref, k_ref, v_ref, bias_ref, lam_ref, subln_ref, o_ref,
                        qs_ref, m_ref, l_ref, acc_ref, *, g, tq, scale):
    qi, ki = qi_ref[pl.program_id(2)], ki_ref[pl.program_id(2)]

    @pl.when(ki == 0)
    def _():
        q = q_ref[...] * scale
        ones, twos = [], []
        for i in range(g):
            a, b = _split_components(q[:, i * HEAD_DIM:(i + 1) * HEAD_DIM])
            ones.append(a)
            twos.append(b)
        qs_ref[...] = jnp.concatenate(ones + twos, axis=0).astype(BF16)
        m_ref[...] = jnp.full(m_ref.shape, NEG, F32)
        l_ref[...] = jnp.zeros(l_ref.shape, F32)
        acc_ref[...] = jnp.zeros(acc_ref.shape, F32)

    st = _dot_t(k_ref[...].astype(BF16), qs_ref[...])
    st = st + jnp.concatenate([bias_ref[0, 0, i] for i in range(g)] * 2, axis=1)
    m_old = m_ref[...]
    m_new = jnp.maximum(m_old, jnp.max(st, axis=0, keepdims=True))
    alpha = jnp.exp(m_old - m_new)
    p = jnp.exp(st - m_new)
    l_ref[...] = alpha * l_ref[...] + jnp.sum(p, axis=0, keepdims=True)
    vt = v_ref[...].T.astype(BF16)
    acc_ref[...] = alpha * acc_ref[...] + _dot(vt, p.astype(BF16))
    m_ref[...] = m_new

    @pl.when(ki == qi)
    def _():
        ot = acc_ref[...] / l_ref[...]
        half = ot.shape[1] // 2
        lam, lam_init = _diff_lambda(lam_ref)
        d = ot[:, :half] - lam * ot[:, half:]
        y = d * lax.rsqrt(jnp.mean(d * d, axis=0, keepdims=True) + EPS) * subln_ref[...] * (1.0 - lam_init)
        yt = y.T
        for i in range(g):
            o_ref[:, i * HEAD_DIM:(i + 1) * HEAD_DIM] = yt[i * tq:(i + 1) * tq].astype(o_ref.dtype)


def _diff_prompt(z, q_col, k_col, v_col, n_seq, seq_len, bias, lam_p, subln, tq):
    g = N_HEADS // KV_D
    nq = seq_len // tq
    qw = g * HEAD_DIM
    qc, kc, vc = q_col // qw, k_col // HEAD_DIM, v_col // HEAD_DIM
    pairs = [(i, j) for i in range(nq) for j in range(i + 1)]
    qi_tab = jnp.asarray([p[0] for p in pairs], jnp.int32)
    ki_tab = jnp.asarray([p[1] for p in pairs], jnp.int32)
    rows = 2 * g * tq
    kv_spec = lambda c0: pl.BlockSpec((tq, HEAD_DIM), lambda b, h, p, qt, kt: (b * nq + kt[p], c0 + h))
    full = lambda a: pl.BlockSpec(a.shape, lambda b, h, p, qt, kt: (0,) * a.ndim)
    subln_col = subln.reshape(HEAD_DIM, 1)
    grid_spec = pltpu.PrefetchScalarGridSpec(
        num_scalar_prefetch=2,
        grid=(n_seq, KV_D, len(pairs)),
        in_specs=[pl.BlockSpec((tq, qw), lambda b, h, p, qt, kt: (b * nq + qt[p], qc + h)),
                  kv_spec(kc), kv_spec(vc),
                  pl.BlockSpec((1, 1, g, tq, tq), lambda b, h, p, qt, kt: (qt[p] - kt[p], h, 0, 0, 0)),
                  full(lam_p), full(subln_col)],
        out_specs=pl.BlockSpec((tq, qw), lambda b, h, p, qt, kt: (b * nq + qt[p], h)),
        scratch_shapes=[pltpu.VMEM((rows, HEAD_DIM), BF16), pltpu.VMEM((1, rows), F32),
                        pltpu.VMEM((1, rows), F32), pltpu.VMEM((HEAD_DIM, rows), F32)],
    )
    return pl.pallas_call(
        functools.partial(_diff_prompt_kernel, g=g, tq=tq, scale=DH_D ** -0.5),
        grid_spec=grid_spec,
        out_shape=jax.ShapeDtypeStruct((n_seq * seq_len, BRANCH), BF16),
        compiler_params=_cparams("parallel", "parallel", "arbitrary"),
        name="diff_prompt",
    )(qi_tab, ki_tab, z, z, z, bias, lam_p, subln_col)


def _diff_sample_kernel(pt_ref, q_ref, *refs, pps, scale):
    pages = refs[:pps]
    kn_ref, vn_ref, bp_ref, bn_ref, lam_ref, subln_ref, o_ref, m_ref, l_ref, acc_ref = refs[pps:]
    p_idx = pl.program_id(1)
    rw = 2 * KV_D

    @pl.when(p_idx == 0)
    def _():
        m_ref[...] = jnp.full(m_ref.shape, NEG, F32)
        l_ref[...] = jnp.zeros(l_ref.shape, F32)
        acc_ref[...] = jnp.zeros(acc_ref.shape, F32)

    q = (q_ref[0] * scale).astype(BF16)

    def update(k, v, bias):
        s = jnp.einsum("hrd,hkd->hrk", q, k, preferred_element_type=F32) + jnp.concatenate([bias, bias], axis=1)
        m_old = m_ref[...]
        m_new = jnp.maximum(m_old, jnp.max(s, axis=-1, keepdims=True))
        alpha = jnp.exp(m_old - m_new)
        p = jnp.exp(s - m_new)
        l_ref[...] = alpha * l_ref[...] + jnp.sum(p, axis=-1, keepdims=True)
        pv = jnp.einsum("hrk,hkd->hrd", p.astype(BF16), v, preferred_element_type=F32)
        acc_ref[...] = alpha * acc_ref[...] + pv
        m_ref[...] = m_new

    def head_rows(first_row):
        return jnp.stack([jnp.concatenate([r[pl.ds(first_row + h, PAGE, stride=rw), :] for r in pages], axis=0)
                          for h in range(KV_D)]).astype(BF16)

    update(head_rows(0), head_rows(KV_D), bp_ref[...])

    @pl.when(p_idx == pl.num_programs(1) - 1)
    def _():
        update(kn_ref[0].astype(BF16), vn_ref[0].astype(BF16), bn_ref[...])
        for h in range(KV_D):
            o_ref[0, h] = _diff_finish(acc_ref[h], l_ref[h], lam_ref, subln_ref)


def _diff_sample(page_table, q, pool_flat, k_new, v_new, bias_past, bias_new, lam_p, subln):
    n_seq, _, rows2, dh = q.shape
    rows = rows2 // 2
    n_pages = page_table.shape[1]
    pps = _pick(n_pages, (PAGES_PER_STEP, 4, 2, 1))
    n_pad = k_new.shape[2]
    rw = 2 * KV_D
    per_seq = lambda shape: pl.BlockSpec(shape, lambda b, p, pt: (b,) + (0,) * (len(shape) - 1))
    full = lambda shape: pl.BlockSpec(shape, lambda b, p, pt: (0,) * len(shape))
    grid_spec = pltpu.PrefetchScalarGridSpec(
        num_scalar_prefetch=1,
        grid=(n_seq, n_pages // pps),
        in_specs=[per_seq((1, KV_D, rows2, dh))]
        + [pl.BlockSpec((PAGE * rw, dh), functools.partial(lambda b, p, pt, i: (pt[b, p * pps + i], 0), i=i))
           for i in range(pps)]
        + [per_seq((1, KV_D, n_pad, dh)), per_seq((1, KV_D, n_pad, dh)),
           pl.BlockSpec((KV_D, rows, pps * PAGE), lambda b, p, pt: (0, 0, p)),
           full((KV_D, rows, n_pad)), full(lam_p.shape), full(subln.shape)],
        out_specs=per_seq((1, KV_D, rows, dh)),
        scratch_shapes=[pltpu.VMEM((KV_D, rows2, 1), F32), pltpu.VMEM((KV_D, rows2, 1), F32),
                        pltpu.VMEM((KV_D, rows2, dh), F32)],
    )
    return pl.pallas_call(
        functools.partial(_diff_sample_kernel, pps=pps, scale=DH_D ** -0.5),
        grid_spec=grid_spec,
        out_shape=jax.ShapeDtypeStruct((n_seq, KV_D, rows, dh), F32),
        compiler_params=_cparams("parallel", "arbitrary"),
        name="diff_sample",
    )(page_table, q, *([pool_flat] * pps), k_new, v_new, bias_past, bias_new, lam_p, subln)


def _heads_to_rows(x, n_seq, n_new, n_kv):
    g = N_HEADS // n_kv
    x = x.reshape(n_seq, n_new, n_kv, g, HEAD_DIM)
    return jnp.transpose(x, (0, 2, 3, 1, 4)).reshape(n_seq, n_kv, g * n_new, HEAD_DIM)


def _rows_to_heads(x, n_seq, n_new, n_kv):
    g = N_HEADS // n_kv
    x = x.reshape(n_seq, n_kv, g, n_new, HEAD_DIM)
    return jnp.transpose(x, (0, 3, 1, 2, 4)).reshape(n_seq * n_new, BRANCH)


def _new_kv(kv_cols, n_seq, n_new, n_kv, n_pad):
    kv = kv_cols.reshape(n_seq, n_new, 2, n_kv, HEAD_DIM)
    pad = lambda a: jnp.pad(jnp.transpose(a, (0, 2, 1, 3)), ((0, 0), (0, 0), (0, n_pad - n_new), (0, 0)))
    return kv_cols.reshape(n_seq * n_new * 2 * n_kv, HEAD_DIM), pad(kv[:, :, 0]), pad(kv[:, :, 1])


def _gate_first(w, gate_col):
    return jnp.concatenate([w[:, gate_col:gate_col + BRANCH], w[:, :gate_col], w[:, gate_col + BRANCH:]], axis=1)


def _swap_halves(w):
    half = w.shape[-1] // 2
    return jnp.concatenate([w[..., half:], w[..., :half]], axis=-1)


def _rope_tables(pos):
    half = ROPE_DIM // 2
    inv = ROPE_THETA ** (-jnp.arange(half, dtype=F32) / half)
    ang = pos.astype(F32)[:, None] * inv[None, :]
    cos, sin = jnp.cos(ang), jnp.sin(ang)
    return jnp.concatenate([cos, cos, cos, cos], axis=1), jnp.concatenate([-sin, sin, -sin, sin], axis=1)


def kernel(x_prompt, x_sample, cache_a_kv, cache_b_lat, cache_b_kpe, cache_c_kv1, cache_c_kv2, cache_c_kv3, cache_d_kv, page_table, rel_bias, ln_gain, final_gain, a_w_in, a_sink, a_w_out, b_w_in, b_q_norm, b_kv_norm, b_w_uq, b_w_uk, b_w_uv, b_w_out, c_w_in, c_w_out, d_w_in, d_lambda, d_subln, d_w_out):
    nb, t, d = x_prompt.shape
    ns, n_new, _ = x_sample.shape
    assert d == D_MODEL and ln_gain.shape[0] == 4 and t % (max(C_DILATIONS) * BLOCK) == 0
    n_past = page_table.shape[1] * PAGE
    mp, ms = nb * t, ns * n_new
    n_pad = 8
    hp, hs = x_prompt.reshape(mp, d), x_sample.reshape(ms, d)
    f, f_rev = _t5_table(rel_bias, max(n_past + n_new, t) + n_pad)
    bf = lambda w: w.astype(BF16)

    nq, nk = BRANCH, KV_A * HEAD_DIM
    w_in = bf(_gate_first(a_w_in[0], nq + 2 * nk))
    w_out = bf(a_w_out[0])
    zp, zs = _rms_proj(hp, ln_gain[0], w_in), _rms_proj(hs, ln_gain[0], w_in)
    band_bias = {dil: _band_bias(f_rev, dil) for dil in C_DILATIONS}
    o_p = _band_attention(zp, BRANCH, BRANCH + nq, BRANCH + nq + nk, nb, t, KV_A,
                          band_bias[1], _sink_rows(a_sink[0], KV_A, BLOCK), False)
    a_kv_prompt = zp[:, BRANCH + nq:].reshape(nb, t, 2, KV_A, HEAD_DIM)[:, -min(WIN_A, t):][None]
    n_buf = cache_a_kv.shape[2]
    new_flat, k_new, v_new = _new_kv(zs[:, BRANCH + nq:], ns, n_new, KV_A, n_pad)
    bias_buf, bias_new = _decode_bias(f_rev, n_buf, n_new, WIN_A, 1, KV_A, n_pad)
    o_s, cache = _window_decode(_heads_to_rows(zs[:, BRANCH:BRANCH + nq], ns, n_new, KV_A),
                                cache_a_kv[0].reshape(-1, HEAD_DIM), new_flat, k_new, v_new, bias_buf, bias_new,
                                _sink_rows(a_sink[0], KV_A, n_new), KV_A, n_buf, n_new, False)
    a_kv_sample = cache.reshape(cache_a_kv.shape)
    hp = _gated_out(hp, zp, [o_p], [], w_out)
    hs = _gated_out(hs, zs, [_rows_to_heads(o_s, ns, n_new, KV_A)], [], w_out)

    w = b_w_in[0]
    kpe_w = w[:, Q_LORA + KV_LORA:Q_LORA + KV_LORA + ROPE_DIM]
    w_in = bf(jnp.concatenate([w[:, Q_LORA + KV_LORA + ROPE_DIM:], w[:, :Q_LORA + KV_LORA], kpe_w, _swap_halves(kpe_w)], axis=1))
    w_out = bf(b_w_out[0])
    uq = b_w_uq[0]
    w_nope = bf(uq[:, :, :NOPE_DIM].reshape(Q_LORA, N_HEADS * NOPE_DIM))
    pad_pe = lambda a: jnp.pad(a, ((0, 0), (0, 0), (0, 128 - ROPE_DIM))).reshape(Q_LORA, N_HEADS * 128)
    w_pe, w_pe_sw = bf(pad_pe(uq[:, :, NOPE_DIM:])), bf(pad_pe(_swap_halves(uq[:, :, NOPE_DIM:])))
    w_ukt = bf(jnp.transpose(b_w_uk[0], (1, 2, 0)))
    w_uv = bf(jnp.transpose(b_w_uv[0], (1, 0, 2)))
    zp, zs = _rms_proj(hp, ln_gain[1], w_in), _rms_proj(hs, ln_gain[1], w_in)
    cos_p, sin_p = _rope_tables(jnp.arange(t))
    cos_s, sin_s = _rope_tables(n_past + jnp.arange(n_new))
    tmq = _pick(mp, (256, 128, 64, 32, 16, 8))
    assert t % tmq == 0
    ql_p, qp_p, lat_p, kpe_p, latb_p, kpeb_p, latt_p = _mla_q(zp, cos_p, sin_p, b_q_norm[0], b_kv_norm[0],
                                                               w_nope, w_pe, w_pe_sw, w_ukt, t // tmq)
    ql_s, qp_s, lat_s, kpe_s, _, _, _ = _mla_q(zs, jnp.tile(cos_s, (ns, 1)), jnp.tile(sin_s, (ns, 1)), b_q_norm[0],
                                                b_kv_norm[0], w_nope, w_pe, w_pe_sw, w_ukt, ms)
    o_p = _mla_prompt(ql_p, qp_p, latb_p, latt_p, kpeb_p, w_uv, nb, t)
    to_rows = lambda a, w_: jnp.transpose(a.reshape(ns, n_new, N_HEADS, w_), (0, 2, 1, 3)).reshape(ns, N_HEADS * n_new, w_)
    qls = to_rows(ql_s, KV_LORA)
    qps = to_rows(qp_s.reshape(ms, N_HEADS, 128)[:, :, :ROPE_DIM].reshape(ms, N_HEADS * ROPE_DIM), ROPE_DIM)
    padn = lambda a: jnp.pad(a.reshape(ns, n_new, -1), ((0, 0), (0, n_pad - n_new), (0, 0)))
    ii = jnp.arange(N_HEADS * n_new)[:, None] % n_new
    mask_new = jnp.where(jnp.arange(n_pad)[None, :] <= ii, 0.0, NEG).astype(F32)
    ol_s = _mla_sample(page_table, qls, qps, cache_b_lat[0], jnp.swapaxes(cache_b_kpe[0], -1, -2),
                       padn(lat_s), padn(kpe_s), mask_new)
    ol_s = jnp.transpose(ol_s.reshape(ns, N_HEADS, n_new, KV_LORA), (0, 2, 1, 3)).reshape(ms, N_HEADS * KV_LORA)
    hp = _gated_out(hp, zp, [o_p], [], w_out)
    hs = _gated_out(hs, zs, [_mla_o(ol_s, w_uv)], [], w_out)
    b_lat_prompt, b_lat_sample = lat_p.reshape(1, nb, t, KV_LORA), lat_s.reshape(1, ns, n_new, KV_LORA)
    b_kpe_prompt, b_kpe_sample = kpe_p.reshape(1, nb, t, ROPE_DIM), kpe_s.reshape(1, ns, n_new, ROPE_DIM)

    n_grp = len(C_WINDOWS)
    gw = BRANCH + 2 * KV_C * HEAD_DIM
    w_in = bf(_gate_first(c_w_in[0], n_grp * gw))
    w_out = bf(c_w_out[0])
    zp, zs = _rms_proj(hp, ln_gain[2], w_in), _rms_proj(hs, ln_gain[2], w_in)
    outs_p, lses_p, outs_s, lses_s, c_prompt, c_sample = [], [], [], [], [], []
    for gi, (win, dil, buf) in enumerate(zip(C_WINDOWS, C_DILATIONS, (cache_c_kv1, cache_c_kv2, cache_c_kv3))):
        c0 = BRANCH + gi * gw
        kv_p = zp[:, c0 + BRANCH:c0 + gw]
        c_prompt.append(kv_p.reshape(nb, t, 2, KV_C, HEAD_DIM)[:, -min(win, t):][None])
        assert win // dil == BLOCK
        bias = band_bias[dil]
        cols = (c0, c0 + BRANCH, c0 + BRANCH + KV_C * HEAD_DIM)
        if dil == 1:
            o, lse = _band_attention(zp, *cols, nb, t, KV_C, bias, None, True)
        else:
            o, lse = _dilated_band_attention(zp, *cols, nb, t, KV_C, dil, bias)
        outs_p.append(o)
        lses_p.append(lse)
        n_buf = buf.shape[2]
        new_flat, k_new, v_new = _new_kv(zs[:, c0 + BRANCH:c0 + gw], ns, n_new, KV_C, n_pad)
        bias_buf, bias_new = _decode_bias(f_rev, n_buf, n_new, win, dil, KV_C, n_pad)
        o, lse, cache = _window_decode(_heads_to_rows(zs[:, c0:c0 + BRANCH], ns, n_new, KV_C),
                                       buf[0].reshape(-1, HEAD_DIM), new_flat, k_new, v_new, bias_buf, bias_new,
                                       None, KV_C, n_buf, n_new, True)
        outs_s.append(_rows_to_heads(o, ns, n_new, KV_C))
        lses_s.append(_rows_to_heads(lse, ns, n_new, KV_C))
        c_sample.append(cache.reshape(buf.shape))
    hp = _gated_out(hp, zp, outs_p, lses_p, w_out)
    hs = _gated_out(hs, zs, outs_s, lses_s, w_out)

    nkv = 2 * KV_D * HEAD_DIM
    w_in = bf(_gate_first(d_w_in[0], BRANCH + nkv))
    w_out = bf(d_w_out[0])
    zp, zs = _rms_proj(hp, ln_gain[3], w_in), _rms_proj(hs, ln_gain[3], w_in)
    lam_p, subln = d_lambda[0].astype(F32), d_subln[0].reshape(1, HEAD_DIM).astype(F32)
    tq = _pick(t, (256, 128))
    o_p = _diff_prompt(zp, BRANCH, 2 * BRANCH, 2 * BRANCH + KV_D * HEAD_DIM, nb, t,
                       _causal_bias_t(f[:, :t], tq).reshape(t // tq, KV_D, N_HEADS // KV_D, tq, tq), lam_p, subln, tq)
    d_kv_prompt = zp[:, 2 * BRANCH:].reshape(1, nb, t, 2, KV_D, HEAD_DIM)
    d_kv_sample = zs[:, 2 * BRANCH:].reshape(1, ns, n_new, 2, KV_D, HEAD_DIM)
    _, k_new, v_new = _new_kv(zs[:, 2 * BRANCH:], ns, n_new, KV_D, n_pad)
    bias_past, bias_new = _decode_bias(f_rev, n_past, n_new, n_past + n_new, 1, KV_D, n_pad)
    q = _heads_to_rows(zs[:, BRANCH:2 * BRANCH], ns, n_new, KV_D)
    q = jnp.concatenate(_split_components(q), axis=2)
    o_s = _diff_sample(page_table, q, cache_d_kv[0].reshape(-1, HEAD_DIM), k_new, v_new, bias_past, bias_new, lam_p, subln)
    hp = _gated_out(hp, zp, [o_p], [], w_out)
    hs = _gated_out(hs, zs, [_rows_to_heads(o_s, ns, n_new, KV_D)], [], w_out)

    y_prompt = _rms_final(hp, final_gain).reshape(nb, t, d)
    y_sample = _rms_final(hs, final_gain).reshape(ns, n_new, d)
    return (y_prompt, y_sample, a_kv_prompt, a_kv_sample, b_lat_prompt, b_lat_sample, b_kpe_prompt, b_kpe_sample,
            c_prompt[0], c_sample[0], c_prompt[1], c_sample[1], c_prompt[2], c_sample[2], d_kv_prompt, d_kv_sample)
```

```python
import functools
import math

import jax
import jax.numpy as jnp
from jax import lax
from jax.experimental import pallas as pl
from jax.experimental.pallas import tpu as pltpu

F32 = jnp.float32
BF16 = jnp.bfloat16
NEG = -1e30
EPS = 1e-6

D_MODEL = 2048
HEAD_DIM = 128
N_HEADS = 16
BRANCH = N_HEADS * HEAD_DIM
PAGE = 128
BLOCK = 128
N_BUCKETS = 32
T5_MAX_DISTANCE = 2048
WIN_A, KV_A = 128, 2
Q_LORA, KV_LORA, NOPE_DIM, ROPE_DIM = 512, 512, 128, 64
ROPE_THETA = 10000.0
C_WINDOWS, C_DILATIONS, KV_C = (128, 512, 2048), (1, 4, 16), 4
DH_D, KV_D = 64, 4
D_LAYER = 3
PAGES_PER_STEP = 16
VMEM_LIMIT_BYTES = 56 * 1024 * 1024


def _cparams(*sem):
    return pltpu.CompilerParams(dimension_semantics=sem, vmem_limit_bytes=VMEM_LIMIT_BYTES)


def _pick(n, cands):
    for c in cands:
        if n % c == 0:
            return c
    raise ValueError(f"no tile for {n}")


_ROW_TILES = (512, 256, 128, 64, 32, 16, 8)


def _dot_t(a, b):
    return lax.dot_general(a, b, (((1,), (1,)), ((), ())), preferred_element_type=F32)


def _dot(a, b):
    return jnp.dot(a, b, preferred_element_type=F32)


def _rms_proj_kernel(x_ref, g_ref, w_ref, o_ref, xn_ref):
    @pl.when(pl.program_id(1) == 0)
    def _():
        x = x_ref[...]
        y = x * lax.rsqrt(jnp.mean(x * x, axis=-1, keepdims=True) + EPS) * g_ref[...]
        xn_ref[...] = y.astype(BF16)

    o_ref[...] = _dot(xn_ref[...], w_ref[...])


def _rms_proj(x, g, w):
    m, d = x.shape
    n = w.shape[1]
    tm = _pick(m, (1024,) + _ROW_TILES)
    tn = _pick(n, (1024, 768, 640, 512, 384, 256, 128))
    return pl.pallas_call(
        _rms_proj_kernel,
        grid=(m // tm, n // tn),
        in_specs=[pl.BlockSpec((tm, d), lambda i, j: (i, 0)),
                  pl.BlockSpec((1, d), lambda i, j: (0, 0)),
                  pl.BlockSpec((d, tn), lambda i, j: (0, j))],
        out_specs=pl.BlockSpec((tm, tn), lambda i, j: (i, j)),
        out_shape=jax.ShapeDtypeStruct((m, n), F32),
        scratch_shapes=[pltpu.VMEM((tm, d), BF16)],
        compiler_params=_cparams("parallel", "arbitrary"),
        name="rms_proj",
    )(x, g.reshape(1, d), w)


def _gated_out_kernel(*refs, n_groups, final_norm):
    h_ref, gate_ref, w_ref = refs[:3]
    o_ref = refs[-1]
    parts = refs[4:-1] if final_norm else refs[3:-1]
    if n_groups == 1:
        o = parts[0][...].astype(F32)
    else:
        outs = [r[...] for r in parts[:n_groups]]
        lses = [r[...] for r in parts[n_groups:]]
        m = functools.reduce(jnp.maximum, lses)
        es = [jnp.exp(l - m) for l in lses]
        den = functools.reduce(lambda a, b: a + b, es)
        o = functools.reduce(lambda a, b: a + b, [e * x for e, x in zip(es, outs)]) / den
    gate = gate_ref[...]
    a = (o * (gate * jax.nn.sigmoid(gate))).astype(BF16)
    y = h_ref[...] + _dot(a, w_ref[...])
    if final_norm:
        y = y * lax.rsqrt(jnp.mean(y * y, axis=-1, keepdims=True) + EPS) * refs[3][...]
    o_ref[...] = y


def _gated_out(h, z, outs, lses, w_out, final_gain=None):
    m, d = h.shape
    n_groups = len(outs)
    tm = _pick(m, _ROW_TILES if n_groups == 1 else _ROW_TILES[1:])
    parts = list(outs) + (list(lses) if n_groups > 1 else [])
    row = pl.BlockSpec((tm, BRANCH), lambda i: (i, 0))
    final = [] if final_gain is None else [final_gain.reshape(1, d)]
    return pl.pallas_call(
        functools.partial(_gated_out_kernel, n_groups=n_groups, final_norm=final_gain is not None),
        grid=(m // tm,),
        in_specs=[pl.BlockSpec((tm, d), lambda i: (i, 0)), row,
                  pl.BlockSpec((BRANCH, d), lambda i: (0, 0), pipeline_mode=pl.Buffered(1))]
        + [pl.BlockSpec((1, d), lambda i: (0, 0))] * len(final) + [row] * len(parts),
        out_specs=pl.BlockSpec((tm, d), lambda i: (i, 0)),
        out_shape=jax.ShapeDtypeStruct((m, d), F32),
        compiler_params=_cparams("parallel"),
        name="gated_out",
    )(h, z, w_out, *final, *parts)


def _t5_table(rel_bias, n_dist):
    exact = N_BUCKETS // 2
    dist = jnp.arange(n_dist)
    df = jnp.maximum(dist, 1).astype(F32)
    far = exact + (jnp.log(df / exact) / math.log(T5_MAX_DISTANCE / exact) * (N_BUCKETS - exact)).astype(jnp.int32)
    bucket = jnp.where(dist < exact, dist, jnp.minimum(far, N_BUCKETS - 1))
    return rel_bias[bucket].astype(F32).T, rel_bias[bucket[::-1]].astype(F32).T


def _toeplitz(u, rows, cols):
    p = u.shape[-1]
    lead = u.shape[:-1]
    assert p >= rows + cols
    flat = jnp.tile(u, (1,) * len(lead) + (rows,))[..., :rows * (p - 1)]
    return flat.reshape(lead + (rows, p - 1))[..., :cols]


def _neg(*shape):
    return jnp.full(shape, NEG, F32)


def _band_bias(f_rev, dil):
    n = f_rev.shape[1]
    val_rev = f_rev[:, n - 1 - BLOCK * dil::dil][:, :BLOCK + 1]
    u = jnp.concatenate([val_rev, _neg(N_HEADS, 3 * BLOCK - 1)], axis=1)
    return _toeplitz(u, BLOCK, 2 * BLOCK)


def _band_bias_t(f, dil):
    far = f[:, BLOCK * dil:BLOCK * dil + 1]
    u = jnp.concatenate([far, _neg(N_HEADS, 3 * BLOCK - 1), f[:, :BLOCK * dil:dil]], axis=1)
    return _toeplitz(u, 2 * BLOCK, BLOCK)


def _causal_bias_rows(f, tq):
    n_delta = f.shape[1] // tq
    fp = jnp.concatenate([_neg(N_HEADS, tq), f, _neg(N_HEADS, tq)], axis=1)
    return jnp.stack([jnp.concatenate([fp[:, tq + d * tq:2 * tq + d * tq], fp[:, d * tq:tq + d * tq]], axis=1)
                      for d in range(n_delta)])


def _decode_bias(f_rev, n_past, n_new, window, dil, n_kv, n_pad):
    g = N_HEADS // n_kv
    n = n_past + n_new + 1
    dd = n - 1 - jnp.arange(n)
    rev = jnp.where(((dd <= window) & (dd % dil == 0))[None, :], f_rev[:, f_rev.shape[1] - n:], NEG)
    u = jnp.concatenate([rev[:, n_new:], _neg(N_HEADS, n_pad), rev[:, :n_new]], axis=1)
    t = _toeplitz(u, n_new, n_past + n_pad).reshape(n_kv, g * n_new, n_past + n_pad)
    return t[..., :n_past], t[..., n_past:]


def _band_kernel(*refs, n_kv, has_sink, want_lse, scale):
    q_refs, (kp_ref, kc_ref, vp_ref, vc_ref, bias_ref) = refs[:2], refs[2:7]
    rest = refs[7:]
    if has_sink:
        sink_ref, rest = rest[0], rest[1:]
    o_ref = rest[0]
    g = N_HEADS // n_kv
    half = N_HEADS // 2

    def q_head(head):
        c = (head % half) * HEAD_DIM
        return q_refs[head // half][:, c:c + HEAD_DIM]

    first = pl.program_id(1) == 0
    for h in range(n_kv):
        cols = slice(h * HEAD_DIM, (h + 1) * HEAD_DIM)
        qs = jnp.concatenate([q_head(h * g + i) for i in range(g)], axis=0)
        qs = (qs * scale).astype(BF16)
        k = jnp.concatenate([kp_ref[:, cols], kc_ref[:, cols]], axis=0).astype(BF16)
        v = jnp.concatenate([vp_ref[:, cols], vc_ref[:, cols]], axis=0)
        if want_lse:
            s = _dot_t(qs, k) + jnp.concatenate([bias_ref[h * g + i] for i in range(g)], axis=0)
            col = lax.broadcasted_iota(jnp.int32, s.shape, 1)
            s = jnp.where(first & (col < BLOCK), NEG, s)
            m = jnp.max(s, axis=-1, keepdims=True)
            p = jnp.exp(s - m)
            l = jnp.sum(p, axis=-1, keepdims=True)
            o = _dot(p.astype(BF16), v.astype(BF16)) / l
            lse = jnp.broadcast_to(m + jnp.log(l), o.shape)
            for i in range(g):
                c = (h * g + i) * HEAD_DIM
                o_ref[:, c:c + HEAD_DIM] = o[i * BLOCK:(i + 1) * BLOCK].astype(o_ref.dtype)
                rest[1][:, c:c + HEAD_DIM] = lse[i * BLOCK:(i + 1) * BLOCK]
            continue
        st = _dot_t(k, qs) + jnp.concatenate([bias_ref[h * g + i] for i in range(g)], axis=1)
        key = lax.broadcasted_iota(jnp.int32, st.shape, 0)
        st = jnp.where(first & (key < BLOCK), NEG, st)
        m = jnp.max(st, axis=0, keepdims=True)
        if has_sink:
            m = jnp.maximum(m, sink_ref[h])
        p = jnp.exp(st - m)
        l = jnp.sum(p, axis=0, keepdims=True)
        if has_sink:
            l = l + jnp.exp(sink_ref[h] - m)
        o = (_dot(v.T.astype(BF16), p.astype(BF16)) / l).T
        for i in range(g):
            c = (h * g + i) * HEAD_DIM
            o_ref[:, c:c + HEAD_DIM] = o[i * BLOCK:(i + 1) * BLOCK].astype(o_ref.dtype)


def _band_attention(src, q_col, k_col, v_col, n_seq, seq_len, n_kv, bias, sink, want_lse):
    g = N_HEADS // n_kv
    nb = seq_len // BLOCK
    qw, kw = BRANCH // 2, n_kv * HEAD_DIM
    qc, kc, vc = q_col // qw, k_col // kw, v_col // kw
    cur = lambda w, c0: pl.BlockSpec((BLOCK, w), lambda b, j: (b * nb + j, c0))
    prev = lambda w, c0: pl.BlockSpec((BLOCK, w), lambda b, j: (b * nb + jnp.maximum(j - 1, 0), c0))
    full = lambda a: pl.BlockSpec(a.shape, lambda b, j: (0,) * a.ndim)
    in_specs = [cur(qw, qc), cur(qw, qc + 1), prev(kw, kc), cur(kw, kc), prev(kw, vc), cur(kw, vc), full(bias)]
    args = [src, src, src, src, src, src, bias]
    if sink is not None:
        in_specs.append(full(sink))
        args.append(sink)
    out_spec = pl.BlockSpec((BLOCK, BRANCH), lambda b, j: (b * nb + j, 0))
    out_sds = jax.ShapeDtypeStruct((n_seq * seq_len, BRANCH), F32 if want_lse else BF16)
    return pl.pallas_call(
        functools.partial(_band_kernel, n_kv=n_kv, has_sink=sink is not None, want_lse=want_lse, scale=HEAD_DIM ** -0.5),
        grid=(n_seq, nb),
        in_specs=in_specs,
        out_specs=[out_spec, out_spec] if want_lse else out_spec,
        out_shape=[out_sds, out_sds] if want_lse else out_sds,
        compiler_params=_cparams("parallel", "arbitrary"),
        name="band_attention",
    )(*args)


def _dilated_band_kernel(q_ref, kp_ref, kc_ref, vp_ref, vc_ref, bias_ref, o_ref, lse_ref, *, dil, scale):
    first = pl.program_id(1) == 0
    bias = bias_ref[0]
    for ph in range(dil):
        rows = pl.ds(ph, BLOCK, stride=dil)
        q = (q_ref[rows, :] * scale).astype(BF16)
        k = jnp.concatenate([kp_ref[rows, :], kc_ref[rows, :]], axis=0).astype(BF16)
        v = jnp.concatenate([vp_ref[rows, :], vc_ref[rows, :]], axis=0).astype(BF16)
        s = _dot_t(q, k) + bias
        col = lax.broadcasted_iota(jnp.int32, s.shape, 1)
        s = jnp.where(first & (col < BLOCK), NEG, s)
        m = jnp.max(s, axis=-1, keepdims=True)
        p = jnp.exp(s - m)
        l = jnp.sum(p, axis=-1, keepdims=True)
        o = _dot(p.astype(BF16), v) / l
        o_ref[rows, :] = o
        lse_ref[rows, :] = jnp.broadcast_to(m + jnp.log(l), o.shape)


def _dilated_band_attention(src, q_col, k_col, v_col, n_seq, seq_len, n_kv, dil, bias):
    g = N_HEADS // n_kv
    span = dil * BLOCK
    n_span = seq_len // span
    qc, kc, vc = q_col // HEAD_DIM, k_col // HEAD_DIM, v_col // HEAD_DIM
    cur = lambda c0, per: pl.BlockSpec((span, HEAD_DIM), lambda b, j, h: (b * n_span + j, c0 + h // per))
    prev = lambda c0, per: pl.BlockSpec((span, HEAD_DIM),
                                        lambda b, j, h: (b * n_span + jnp.maximum(j - 1, 0), c0 + h // per))
    out_spec = cur(0, 1)
    out_sds = jax.ShapeDtypeStruct((n_seq * seq_len, BRANCH), F32)
    return pl.pallas_call(
        functools.partial(_dilated_band_kernel, dil=dil, scale=HEAD_DIM ** -0.5),
        grid=(n_seq, n_span, N_HEADS),
        in_specs=[cur(qc, 1), prev(kc, g), cur(kc, g), prev(vc, g), cur(vc, g),
                  pl.BlockSpec((1, BLOCK, 2 * BLOCK), lambda b, j, h: (h, 0, 0))],
        out_specs=[out_spec, out_spec],
        out_shape=[out_sds, out_sds],
        compiler_params=_cparams("parallel", "arbitrary", "arbitrary"),
        name="dilated_band_attention",
    )(src, src, src, src, src, bias)


def _sink_lanes(sink, n_kv, rows_per_head):
    g = N_HEADS // n_kv
    return jnp.broadcast_to(sink.astype(F32).reshape(n_kv, g, 1), (n_kv, g, rows_per_head)).reshape(n_kv, 1, g * rows_per_head)


def _sink_rows(sink, n_kv, rows_per_head):
    g = N_HEADS // n_kv
    return jnp.broadcast_to(sink.astype(F32).reshape(n_kv, g, 1), (n_kv, g, rows_per_head)).reshape(n_kv, g * rows_per_head, 1)


def _window_decode_kernel(*refs, n_kv, n_past, n_new, sps, has_sink, want_lse, scale):
    q_ref, buf_ref, new_ref, kn_ref, vn_ref, bb_ref, bn_ref = refs[:7]
    rest = refs[7:]
    if has_sink:
        sink_ref, rest = rest[0], rest[1:]
    o_ref, cache_ref = rest[0], rest[-1]
    rw = 2 * n_kv
    keep = (n_past - n_new) * rw
    for s in range(sps):
        base = s * n_past * rw
        cache_ref[pl.ds(base, keep), :] = buf_ref[pl.ds(base + n_new * rw, keep), :]
        cache_ref[pl.ds(base + keep, n_new * rw), :] = new_ref[pl.ds(s * n_new * rw, n_new * rw), :]
        for h in range(n_kv):
            kb = buf_ref[pl.ds(base + h, n_past, stride=rw), :].astype(BF16)
            vb = buf_ref[pl.ds(base + n_kv + h, n_past, stride=rw), :].astype(BF16)
            q = (q_ref[s, h] * scale).astype(BF16)
            s1 = _dot_t(q, kb) + bb_ref[h]
            s2 = _dot_t(q, kn_ref[s, h].astype(BF16)) + bn_ref[h]
            m = jnp.maximum(jnp.max(s1, axis=-1, keepdims=True), jnp.max(s2, axis=-1, keepdims=True))
            if has_sink:
                m = jnp.maximum(m, sink_ref[h])
            p1 = jnp.exp(s1 - m)
            p2 = jnp.exp(s2 - m)
            l = jnp.sum(p1, axis=-1, keepdims=True) + jnp.sum(p2, axis=-1, keepdims=True)
            if has_sink:
                l = l + jnp.exp(sink_ref[h] - m)
            o = (_dot(p1.astype(BF16), vb) + _dot(p2.astype(BF16), vn_ref[s, h].astype(BF16))) / l
            o_ref[s, h] = o
            if want_lse:
                rest[1][s, h] = jnp.broadcast_to(m + jnp.log(l), o.shape)


def _window_decode(q, buf_flat, new_flat, k_new, v_new, bias_buf, bias_new, sink, n_kv, n_past, n_new, want_lse):
    n_seq, _, rows, dh = q.shape
    rw = 2 * n_kv
    n_pad = k_new.shape[2]
    sps = _pick(n_seq, [c for c in (8, 4, 2, 1) if c * n_past <= 1024 or c == 1])
    full = lambda shape: pl.BlockSpec(shape, lambda b: (0,) * len(shape))
    per_seq = lambda shape: pl.BlockSpec(shape, lambda b: (b,) + (0,) * (len(shape) - 1))
    in_specs = [per_seq((sps, n_kv, rows, dh)), per_seq((sps * n_past * rw, dh)), per_seq((sps * n_new * rw, dh)),
                per_seq((sps, n_kv, n_pad, dh)), per_seq((sps, n_kv, n_pad, dh)),
                full((n_kv, rows, n_past)), full((n_kv, rows, n_pad))]
    args = [q, buf_flat, new_flat, k_new, v_new, bias_buf, bias_new]
    if sink is not None:
        in_specs.append(full((n_kv, rows, 1)))
        args.append(sink)
    o_spec, o_sds = per_seq((sps, n_kv, rows, dh)), jax.ShapeDtypeStruct(q.shape, F32)
    out_specs = [o_spec] + ([o_spec] if want_lse else []) + [per_seq((sps * n_past * rw, dh))]
    out_shape = [o_sds] + ([o_sds] if want_lse else []) + [jax.ShapeDtypeStruct(buf_flat.shape, F32)]
    return pl.pallas_call(
        functools.partial(_window_decode_kernel, n_kv=n_kv, n_past=n_past, n_new=n_new, sps=sps,
                          has_sink=sink is not None, want_lse=want_lse, scale=dh ** -0.5),
        grid=(n_seq // sps,),
        in_specs=in_specs,
        out_specs=out_specs,
        out_shape=out_shape,
        compiler_params=_cparams("parallel"),
        name="window_decode",
    )(*args)


def _mla_q_kernel(cq_ref, ckv_ref, kpe_ref, cos_ref, sin_ref, qn_ref, kvn_ref, wn_ref, wp_ref, wps_ref, wuk_ref,
                  ql_ref, qp_ref, lat_ref, kpo_ref, latb_ref, kpb_ref, latt_ref, *, scale):
    def rms(x, g):
        return x * lax.rsqrt(jnp.mean(x * x, axis=-1, keepdims=True) + EPS) * g

    cq = rms(cq_ref[...], qn_ref[...]).astype(BF16)
    lat = rms(ckv_ref[...], kvn_ref[...])
    lat_ref[...] = lat
    latb_ref[...] = lat.astype(BF16)
    latt_ref[...] = lat.T.astype(BF16)
    cos, sin = cos_ref[...], sin_ref[...]
    kp2 = kpe_ref[...]
    kp = kp2 * cos + pltpu.roll(kp2, ROPE_DIM, 1) * sin
    kpo_ref[...] = kp[:, :ROPE_DIM]
    lane = lax.broadcasted_iota(jnp.int32, kp.shape, 1)
    kpb_ref[...] = jnp.where(lane < ROPE_DIM, kp, 0.0).astype(BF16)
    n_rep = wp_ref.shape[1] // cos.shape[1]
    qp = _dot(cq, wp_ref[...]) * jnp.tile(cos, (1, n_rep)) + _dot(cq, wps_ref[...]) * jnp.tile(sin, (1, n_rep))
    qp_ref[...] = (qp * scale).astype(BF16)
    qn = _dot(cq, wn_ref[...])
    for h in range(N_HEADS):
        qh = qn[:, h * NOPE_DIM:(h + 1) * NOPE_DIM].astype(BF16)
        ql_ref[:, h * KV_LORA:(h + 1) * KV_LORA] = (_dot(qh, wuk_ref[h]) * scale).astype(BF16)


def _mla_q(z, cos, sin, q_norm, kv_norm, w_nope, w_pe, w_pe_sw, w_ukt, pos_tiles):
    m = z.shape[0]
    tm = _pick(m, (256, 128, 64, 32, 16, 8))
    c0 = BRANCH // Q_LORA
    full = lambda a: pl.BlockSpec(a.shape, lambda i: (0,) * a.ndim)
    row = lambda w: pl.BlockSpec((tm, w), lambda i: (i, 0))
    pos = pl.BlockSpec((tm, 128), lambda i: (i % pos_tiles, 0))
    qn, kvn = q_norm.reshape(1, Q_LORA), kv_norm.reshape(1, KV_LORA)
    return pl.pallas_call(
        functools.partial(_mla_q_kernel, scale=(NOPE_DIM + ROPE_DIM) ** -0.5),
        grid=(m // tm,),
        in_specs=[pl.BlockSpec((tm, Q_LORA), lambda i: (i, c0)),
                  pl.BlockSpec((tm, KV_LORA), lambda i: (i, c0 + 1)),
                  pl.BlockSpec((tm, 128), lambda i: (i, (BRANCH + Q_LORA + KV_LORA) // 128)),
                  pos, pos, full(qn), full(kvn), full(w_nope), full(w_pe), full(w_pe_sw), full(w_ukt)],
        out_specs=[row(N_HEADS * KV_LORA), row(N_HEADS * 128), row(KV_LORA), row(ROPE_DIM), row(KV_LORA), row(128),
                   pl.BlockSpec((KV_LORA, tm), lambda i: (0, i))],
        out_shape=[jax.ShapeDtypeStruct((m, N_HEADS * KV_LORA), BF16),
                   jax.ShapeDtypeStruct((m, N_HEADS * 128), BF16),
                   jax.ShapeDtypeStruct((m, KV_LORA), F32),
                   jax.ShapeDtypeStruct((m, ROPE_DIM), F32),
                   jax.ShapeDtypeStruct((m, KV_LORA), BF16),
                   jax.ShapeDtypeStruct((m, 128), BF16),
                   jax.ShapeDtypeStruct((KV_LORA, m), BF16)],
        compiler_params=_cparams("parallel"),
        name="mla_q",
    )(z, z, z, cos, sin, qn, kvn, w_nope, w_pe, w_pe_sw, w_ukt)


def _mla_prompt_kernel(qi_ref, ki_ref, ql_ref, qp_ref, c_ref, ct_ref, kp_ref, wuv_ref, o_ref,
                       qs_ref, qps_ref, m_ref, l_ref, acc_ref, *, tq, tk):
    qi, ki = qi_ref[pl.program_id(1)], ki_ref[pl.program_id(1)]
    last = (qi * tq + tq - 1) // tk

    @pl.when(ki == 0)
    def _():
        for h in range(N_HEADS):
            qs_ref[h * tq:(h + 1) * tq, :] = ql_ref[:, h * KV_LORA:(h + 1) * KV_LORA]
            qps_ref[h * tq:(h + 1) * tq, :] = qp_ref[:, h * 128:(h + 1) * 128]
        m_ref[...] = jnp.full(m_ref.shape, NEG, F32)
        l_ref[...] = jnp.zeros(l_ref.shape, F32)
        acc_ref[...] = jnp.zeros(acc_ref.shape, F32)

    def step(masked):
        st = _dot_t(c_ref[...], qs_ref[...]) + _dot_t(kp_ref[...], qps_ref[...])
        if masked:
            kpos = ki * tk + lax.broadcasted_iota(jnp.int32, st.shape, 0)
            qpos = qi * tq + (lax.broadcasted_iota(jnp.int32, st.shape, 1) & (tq - 1))
            st = jnp.where(kpos <= qpos, st, NEG)
        m_old = m_ref[...]
        m_new = jnp.maximum(m_old, jnp.max(st, axis=0, keepdims=True))
        alpha = jnp.exp(m_old - m_new)
        p = jnp.exp(st - m_new)
        l_ref[...] = alpha * l_ref[...] + jnp.sum(p, axis=0, keepdims=True)
        acc_ref[...] = alpha * acc_ref[...] + _dot(ct_ref[...], p.astype(BF16))
        m_ref[...] = m_new

    @pl.when(ki < last)
    def _():
        step(False)

    @pl.when(ki == last)
    def _():
        step(True)
        ot = acc_ref[...] / l_ref[...]
        for h in range(N_HEADS):
            oh = ot[:, h * tq:(h + 1) * tq].T.astype(BF16)
            o_ref[:, h * HEAD_DIM:(h + 1) * HEAD_DIM] = _dot(oh, wuv_ref[h]).astype(o_ref.dtype)


def _mla_prompt(q_lat, q_pe, lat_b, lat_t, kpe_b, w_uv, n_seq, seq_len):
    tq = BLOCK
    tk = _pick(seq_len, (512, 256, 128))
    nq, nk = seq_len // tq, seq_len // tk
    pairs = [(i, j) for i in range(nq) for j in range((i * tq + tq - 1) // tk + 1)]
    qi_tab = jnp.asarray([p[0] for p in pairs], jnp.int32)
    ki_tab = jnp.asarray([p[1] for p in pairs], jnp.int32)
    rows = N_HEADS * tq
    qrow = lambda w: pl.BlockSpec((tq, w), lambda b, p, qt, kt: (b * nq + qt[p], 0))
    krow = lambda w: pl.BlockSpec((tk, w), lambda b, p, qt, kt: (b * nk + kt[p], 0))
    grid_spec = pltpu.PrefetchScalarGridSpec(
        num_scalar_prefetch=2,
        grid=(n_seq, len(pairs)),
        in_specs=[qrow(N_HEADS * KV_LORA), qrow(N_HEADS * 128), krow(KV_LORA),
                  pl.BlockSpec((KV_LORA, tk), lambda b, p, qt, kt: (0, b * nk + kt[p])),
                  krow(128),
                  pl.BlockSpec(w_uv.shape, lambda b, p, qt, kt: (0, 0, 0))],
        out_specs=qrow(BRANCH),
        scratch_shapes=[pltpu.VMEM((rows, KV_LORA), BF16), pltpu.VMEM((rows, 128), BF16),
                        pltpu.VMEM((1, rows), F32), pltpu.VMEM((1, rows), F32),
                        pltpu.VMEM((KV_LORA, rows), F32)],
    )
    return pl.pallas_call(
        functools.partial(_mla_prompt_kernel, tq=tq, tk=tk),
        grid_spec=grid_spec,
        out_shape=jax.ShapeDtypeStruct((n_seq * seq_len, BRANCH), BF16),
        compiler_params=_cparams("parallel", "arbitrary"),
        name="mla_prompt",
    )(qi_tab, ki_tab, q_lat, q_pe, lat_b, lat_t, kpe_b, w_uv)


def _mla_sample_kernel(pt_ref, ql_ref, qp_ref, *refs, pps):
    lat_refs, kpt_refs = refs[:pps], refs[pps:2 * pps]
    cn_ref, kn_ref, mask_ref, o_ref, m_ref, l_ref, acc_ref = refs[2 * pps:]
    p_idx = pl.program_id(1)

    @pl.when(p_idx == 0)
    def _():
        m_ref[...] = jnp.full(m_ref.shape, NEG, F32)
        l_ref[...] = jnp.zeros(l_ref.shape, F32)
        acc_ref[...] = jnp.zeros(acc_ref.shape, F32)

    ql, qp = ql_ref[0], qp_ref[0]

    def update(s, v):
        m_old = m_ref[...]
        m_new = jnp.maximum(m_old, jnp.max(s, axis=-1, keepdims=True))
        alpha = jnp.exp(m_old - m_new)
        p = jnp.exp(s - m_new)
        l_ref[...] = alpha * l_ref[...] + jnp.sum(p, axis=-1, keepdims=True)
        acc_ref[...] = alpha * acc_ref[...] + _dot(p.astype(BF16), v)
        m_ref[...] = m_new

    c = jnp.concatenate([r[0] for r in lat_refs], axis=0).astype(BF16)
    kpt = jnp.concatenate([r[0] for r in kpt_refs], axis=1).astype(BF16)
    update(_dot_t(ql, c) + _dot(qp, kpt), c)

    @pl.when(p_idx == pl.num_programs(1) - 1)
    def _():
        cn = cn_ref[0].astype(BF16)
        update(_dot_t(ql, cn) + _dot_t(qp, kn_ref[0].astype(BF16)) + mask_ref[...], cn)
        o_ref[0] = (acc_ref[...] / l_ref[...]).astype(BF16)


def _mla_sample(page_table, q_lat, q_pe, lat_pool, kpet_pool, c_new, k_new, mask_new):
    n_seq, rows, _ = q_lat.shape
    n_pages = page_table.shape[1]
    pps = _pick(n_pages, (PAGES_PER_STEP, 4, 2, 1))
    n_pad = c_new.shape[1]

    def page(i, shape):
        return pl.BlockSpec(shape, lambda b, p, pt: (pt[b, p * pps + i], 0, 0))

    per_seq = lambda shape: pl.BlockSpec(shape, lambda b, p, pt: (b,) + (0,) * (len(shape) - 1))
    grid_spec = pltpu.PrefetchScalarGridSpec(
        num_scalar_prefetch=1,
        grid=(n_seq, n_pages // pps),
        in_specs=[per_seq((1, rows, KV_LORA)), per_seq((1, rows, ROPE_DIM))]
        + [page(i, (1, PAGE, KV_LORA)) for i in range(pps)]
        + [page(i, (1, ROPE_DIM, PAGE)) for i in range(pps)]
        + [per_seq((1, n_pad, KV_LORA)), per_seq((1, n_pad, ROPE_DIM)),
           pl.BlockSpec((rows, n_pad), lambda b, p, pt: (0, 0))],
        out_specs=per_seq((1, rows, KV_LORA)),
        scratch_shapes=[pltpu.VMEM((rows, 1), F32), pltpu.VMEM((rows, 1), F32), pltpu.VMEM((rows, KV_LORA), F32)],
    )
    return pl.pallas_call(
        functools.partial(_mla_sample_kernel, pps=pps),
        grid_spec=grid_spec,
        out_shape=jax.ShapeDtypeStruct((n_seq, rows, KV_LORA), BF16),
        compiler_params=_cparams("parallel", "arbitrary"),
        name="mla_sample",
    )(page_table, q_lat, q_pe, *([lat_pool] * pps), *([kpet_pool] * pps), c_new, k_new, mask_new)


def _mla_o_kernel(ol_ref, w_ref, o_ref):
    for h in range(N_HEADS):
        o_ref[:, h * HEAD_DIM:(h + 1) * HEAD_DIM] = _dot(ol_ref[:, h * KV_LORA:(h + 1) * KV_LORA], w_ref[h])


def _mla_o(o_lat, w_uv):
    m = o_lat.shape[0]
    tm = _pick(m, (256, 128, 64, 32, 16, 8))
    return pl.pallas_call(
        _mla_o_kernel,
        grid=(m // tm,),
        in_specs=[pl.BlockSpec((tm, N_HEADS * KV_LORA), lambda i: (i, 0)),
                  pl.BlockSpec(w_uv.shape, lambda i: (0, 0, 0))],
        out_specs=pl.BlockSpec((tm, BRANCH), lambda i: (i, 0)),
        out_shape=jax.ShapeDtypeStruct((m, BRANCH), F32),
        compiler_params=_cparams("parallel"),
        name="mla_o",
    )(o_lat, w_uv)


def _diff_lambda(lam_ref):
    lp = lam_ref[...]
    lam_init = 0.8 - 0.6 * math.exp(-0.3 * D_LAYER)
    a = jnp.sum(lp[0:1] * lp[1:2], axis=-1, keepdims=True)
    b = jnp.sum(lp[2:3] * lp[3:4], axis=-1, keepdims=True)
    return jnp.exp(a) - jnp.exp(b) + lam_init, lam_init


def _diff_finish(acc, l, lam_ref, subln_ref):
    half = acc.shape[0] // 2
    lam, lam_init = _diff_lambda(lam_ref)
    o = acc / l
    o = o[:half] - lam * o[half:]
    y = o * lax.rsqrt(jnp.mean(o * o, axis=-1, keepdims=True) + EPS) * subln_ref[...]
    return y * (1.0 - lam_init)


def _split_components(q):
    lane = lax.broadcasted_iota(jnp.int32, q.shape, q.ndim - 1)
    return jnp.where(lane < DH_D, q, 0.0), jnp.where(lane < DH_D, 0.0, q)


def _diff_prompt_kernel(qi_ref, ki_ref, q_ref, k_ref, v_ref, bias_ref, lam_ref, subln_ref, o_ref,
                        qs_ref, m_ref, l_ref, acc_ref, *, g, tq, scale):
    qi, ki = qi_ref[pl.program_id(2)], ki_ref[pl.program_id(2)]

    @pl.when(ki == 0)
    def _():
        q = q_ref[...] * scale
        ones, twos = [], []
        for i in range(g):
            a, b = _split_components(q[:, i * HEAD_DIM:(i + 1) * HEAD_DIM])
            ones.append(a)
            twos.append(b)
        qs_ref[...] = jnp.concatenate(ones + twos, axis=0).astype(BF16)
        m_ref[...] = jnp.full(m_ref.shape, NEG, F32)
        l_ref[...] = jnp.zeros(l_ref.shape, F32)
        acc_ref[...] = jnp.zeros(acc_ref.shape, F32)

    st = _dot_t(k_ref[...].astype(BF16), qs_ref[...])
    tiles = [pltpu.roll(jnp.broadcast_to(bias_ref[0, 0, i], (tq, 2 * tq)), 0, 1, stride=1, stride_axis=0)[:, :tq]
             for i in range(g)]
    st = st + jnp.concatenate(tiles * 2, axis=1)
    m_old = m_ref[...]
    m_new = jnp.maximum(m_old, jnp.max(st, axis=0, keepdims=True))
    alpha = jnp.exp(m_old - m_new)
    p = jnp.exp(st - m_new)
    l_ref[...] = alpha * l_ref[...] + jnp.sum(p, axis=0, keepdims=True)
    vt = v_ref[...].T.astype(BF16)
    acc_ref[...] = alpha * acc_ref[...] + _dot(vt, p.astype(BF16))
    m_ref[...] = m_new

    @pl.when(ki == qi)
    def _():
        ot = acc_ref[...] / l_ref[...]
        half = ot.shape[1] // 2
        lam, lam_init = _diff_lambda(lam_ref)
        d = ot[:, :half] - lam * ot[:, half:]
        y = d * lax.rsqrt(jnp.mean(d * d, axis=0, keepdims=True) + EPS) * subln_ref[...] * (1.0 - lam_init)
        yt = y.T
        for i in range(g):
            o_ref[:, i * HEAD_DIM:(i + 1) * HEAD_DIM] = yt[i * tq:(i + 1) * tq].astype(o_ref.dtype)


def _diff_prompt(z, q_col, k_col, v_col, n_seq, seq_len, bias, lam_p, subln, tq):
    g = N_HEADS // KV_D
    nq = seq_len // tq
    qw = g * HEAD_DIM
    qc, kc, vc = q_col // qw, k_col // HEAD_DIM, v_col // HEAD_DIM
    pairs = [(i, j) for i in range(nq) for j in range(i + 1)]
    qi_tab = jnp.asarray([p[0] for p in pairs], jnp.int32)
    ki_tab = jnp.asarray([p[1] for p in pairs], jnp.int32)
    rows = 2 * g * tq
    kv_spec = lambda c0: pl.BlockSpec((tq, HEAD_DIM), lambda b, h, p, qt, kt: (b * nq + kt[p], c0 + h))
    full = lambda a: pl.BlockSpec(a.shape, lambda b, h, p, qt, kt: (0,) * a.ndim)
    subln_col = subln.reshape(HEAD_DIM, 1)
    grid_spec = pltpu.PrefetchScalarGridSpec(
        num_scalar_prefetch=2,
        grid=(n_seq, KV_D, len(pairs)),
        in_specs=[pl.BlockSpec((tq, qw), lambda b, h, p, qt, kt: (b * nq + qt[p], qc + h)),
                  kv_spec(kc), kv_spec(vc),
                  pl.BlockSpec((1, 1, g, 1, 2 * tq), lambda b, h, p, qt, kt: (qt[p] - kt[p], h, 0, 0, 0)),
                  full(lam_p), full(subln_col)],
        out_specs=pl.BlockSpec((tq, qw), lambda b, h, p, qt, kt: (b * nq + qt[p], h)),
        scratch_shapes=[pltpu.VMEM((rows, HEAD_DIM), BF16), pltpu.VMEM((1, rows), F32),
                        pltpu.VMEM((1, rows), F32), pltpu.VMEM((HEAD_DIM, rows), F32)],
    )
    return pl.pallas_call(
        functools.partial(_diff_prompt_kernel, g=g, tq=tq, scale=DH_D ** -0.5),
        grid_spec=grid_spec,
        out_shape=jax.ShapeDtypeStruct((n_seq * seq_len, BRANCH), BF16),
        compiler_params=_cparams("parallel", "parallel", "arbitrary"),
        name="diff_prompt",
    )(qi_tab, ki_tab, z, z, z, bias, lam_p, subln_col)


def _diff_sample_kernel(pt_ref, q_ref, *refs, pps, scale):
    pages = refs[:pps]
    kn_ref, vn_ref, bp_ref, bn_ref, lam_ref, subln_ref, o_ref, m_ref, l_ref, acc_ref = refs[pps:]
    p_idx = pl.program_id(1)
    rw = 2 * KV_D

    @pl.when(p_idx == 0)
    def _():
        m_ref[...] = jnp.full(m_ref.shape, NEG, F32)
        l_ref[...] = jnp.zeros(l_ref.shape, F32)
        acc_ref[...] = jnp.zeros(acc_ref.shape, F32)

    q = (q_ref[0] * scale).astype(BF16)

    def update(k, v, bias):
        s = jnp.einsum("hrd,hkd->hrk", q, k, preferred_element_type=F32) + jnp.concatenate([bias, bias], axis=1)
        m_old = m_ref[...]
        m_new = jnp.maximum(m_old, jnp.max(s, axis=-1, keepdims=True))
        alpha = jnp.exp(m_old - m_new)
        p = jnp.exp(s - m_new)
        l_ref[...] = alpha * l_ref[...] + jnp.sum(p, axis=-1, keepdims=True)
        pv = jnp.einsum("hrk,hkd->hrd", p.astype(BF16), v, preferred_element_type=F32)
        acc_ref[...] = alpha * acc_ref[...] + pv
        m_ref[...] = m_new

    def head_rows(first_row):
        return jnp.stack([jnp.concatenate([r[pl.ds(first_row + h, PAGE, stride=rw), :] for r in pages], axis=0)
                          for h in range(KV_D)]).astype(BF16)

    update(head_rows(0), head_rows(KV_D), bp_ref[...])

    @pl.when(p_idx == pl.num_programs(1) - 1)
    def _():
        update(kn_ref[0].astype(BF16), vn_ref[0].astype(BF16), bn_ref[...])
        for h in range(KV_D):
            o_ref[0, h] = _diff_finish(acc_ref[h], l_ref[h], lam_ref, subln_ref)


def _diff_sample(page_table, q, pool_flat, k_new, v_new, bias_past, bias_new, lam_p, subln):
    n_seq, _, rows2, dh = q.shape
    rows = rows2 // 2
    n_pages = page_table.shape[1]
    pps = _pick(n_pages, (PAGES_PER_STEP, 4, 2, 1))
    n_pad = k_new.shape[2]
    rw = 2 * KV_D
    per_seq = lambda shape: pl.BlockSpec(shape, lambda b, p, pt: (b,) + (0,) * (len(shape) - 1))
    full = lambda shape: pl.BlockSpec(shape, lambda b, p, pt: (0,) * len(shape))
    grid_spec = pltpu.PrefetchScalarGridSpec(
        num_scalar_prefetch=1,
        grid=(n_seq, n_pages // pps),
        in_specs=[per_seq((1, KV_D, rows2, dh))]
        + [pl.BlockSpec((PAGE * rw, dh), functools.partial(lambda b, p, pt, i: (pt[b, p * pps + i], 0), i=i))
           for i in range(pps)]
        + [per_seq((1, KV_D, n_pad, dh)), per_seq((1, KV_D, n_pad, dh)),
           pl.BlockSpec((KV_D, rows, pps * PAGE), lambda b, p, pt: (0, 0, p)),
           full((KV_D, rows, n_pad)), full(lam_p.shape), full(subln.shape)],
        out_specs=per_seq((1, KV_D, rows, dh)),
        scratch_shapes=[pltpu.VMEM((KV_D, rows2, 1), F32), pltpu.VMEM((KV_D, rows2, 1), F32),
                        pltpu.VMEM((KV_D, rows2, dh), F32)],
    )
    return pl.pallas_call(
        functools.partial(_diff_sample_kernel, pps=pps, scale=DH_D ** -0.5),
        grid_spec=grid_spec,
        out_shape=jax.ShapeDtypeStruct((n_seq, KV_D, rows, dh), F32),
        compiler_params=_cparams("parallel", "arbitrary"),
        name="diff_sample",
    )(page_table, q, *([pool_flat] * pps), k_new, v_new, bias_past, bias_new, lam_p, subln)


def _heads_to_rows(x, n_seq, n_new, n_kv):
    g = N_HEADS // n_kv
    x = x.reshape(n_seq, n_new, n_kv, g, HEAD_DIM)
    return jnp.transpose(x, (0, 2, 3, 1, 4)).reshape(n_seq, n_kv, g * n_new, HEAD_DIM)


def _rows_to_heads(x, n_seq, n_new, n_kv):
    g = N_HEADS // n_kv
    x = x.reshape(n_seq, n_kv, g, n_new, HEAD_DIM)
    return jnp.transpose(x, (0, 3, 1, 2, 4)).reshape(n_seq * n_new, BRANCH)


def _new_kv(kv_cols, n_seq, n_new, n_kv, n_pad):
    kv = kv_cols.reshape(n_seq, n_new, 2, n_kv, HEAD_DIM)
    pad = lambda a: jnp.pad(jnp.transpose(a, (0, 2, 1, 3)), ((0, 0), (0, 0), (0, n_pad - n_new), (0, 0)))
    return kv_cols.reshape(n_seq * n_new * 2 * n_kv, HEAD_DIM), pad(kv[:, :, 0]), pad(kv[:, :, 1])


def _gate_first(w, gate_col):
    return jnp.concatenate([w[:, gate_col:gate_col + BRANCH], w[:, :gate_col], w[:, gate_col + BRANCH:]], axis=1)


def _swap_halves(w):
    half = w.shape[-1] // 2
    return jnp.concatenate([w[..., half:], w[..., :half]], axis=-1)


def _rope_tables(pos):
    half = ROPE_DIM // 2
    inv = ROPE_THETA ** (-jnp.arange(half, dtype=F32) / half)
    ang = pos.astype(F32)[:, None] * inv[None, :]
    cos, sin = jnp.cos(ang), jnp.sin(ang)
    return jnp.concatenate([cos, cos, cos, cos], axis=1), jnp.concatenate([-sin, sin, -sin, sin], axis=1)


def kernel(x_prompt, x_sample, cache_a_kv, cache_b_lat, cache_b_kpe, cache_c_kv1, cache_c_kv2, cache_c_kv3, cache_d_kv, page_table, rel_bias, ln_gain, final_gain, a_w_in, a_sink, a_w_out, b_w_in, b_q_norm, b_kv_norm, b_w_uq, b_w_uk, b_w_uv, b_w_out, c_w_in, c_w_out, d_w_in, d_lambda, d_subln, d_w_out):
    nb, t, d = x_prompt.shape
    ns, n_new, _ = x_sample.shape
    assert d == D_MODEL and ln_gain.shape[0] == 4 and t % (max(C_DILATIONS) * BLOCK) == 0
    n_past = page_table.shape[1] * PAGE
    mp, ms = nb * t, ns * n_new
    n_pad = 8
    hp, hs = x_prompt.reshape(mp, d), x_sample.reshape(ms, d)
    f, f_rev = _t5_table(rel_bias, max(n_past + n_new, t) + n_pad)
    bf = lambda w: w.astype(BF16)

    nq, nk = BRANCH, KV_A * HEAD_DIM
    w_in = bf(_gate_first(a_w_in[0], nq + 2 * nk))
    w_out = bf(a_w_out[0])
    zp, zs = _rms_proj(hp, ln_gain[0], w_in), _rms_proj(hs, ln_gain[0], w_in)
    band_bias = {dil: _band_bias(f_rev, dil) for dil in C_DILATIONS}
    o_p = _band_attention(zp, BRANCH, BRANCH + nq, BRANCH + nq + nk, nb, t, KV_A,
                          _band_bias_t(f, 1), _sink_lanes(a_sink[0], KV_A, BLOCK), False)
    a_kv_prompt = zp[:, BRANCH + nq:].reshape(nb, t, 2, KV_A, HEAD_DIM)[:, -min(WIN_A, t):][None]
    n_buf = cache_a_kv.shape[2]
    new_flat, k_new, v_new = _new_kv(zs[:, BRANCH + nq:], ns, n_new, KV_A, n_pad)
    bias_buf, bias_new = _decode_bias(f_rev, n_buf, n_new, WIN_A, 1, KV_A, n_pad)
    o_s, cache = _window_decode(_heads_to_rows(zs[:, BRANCH:BRANCH + nq], ns, n_new, KV_A),
                                cache_a_kv[0].reshape(-1, HEAD_DIM), new_flat, k_new, v_new, bias_buf, bias_new,
                                _sink_rows(a_sink[0], KV_A, n_new), KV_A, n_buf, n_new, False)
    a_kv_sample = cache.reshape(cache_a_kv.shape)
    hp = _gated_out(hp, zp, [o_p], [], w_out)
    hs = _gated_out(hs, zs, [_rows_to_heads(o_s, ns, n_new, KV_A)], [], w_out)

    w = b_w_in[0]
    kpe_w = w[:, Q_LORA + KV_LORA:Q_LORA + KV_LORA + ROPE_DIM]
    w_in = bf(jnp.concatenate([w[:, Q_LORA + KV_LORA + ROPE_DIM:], w[:, :Q_LORA + KV_LORA], kpe_w, _swap_halves(kpe_w)], axis=1))
    w_out = bf(b_w_out[0])
    uq = b_w_uq[0]
    w_nope = bf(uq[:, :, :NOPE_DIM].reshape(Q_LORA, N_HEADS * NOPE_DIM))
    pad_pe = lambda a: jnp.pad(a, ((0, 0), (0, 0), (0, 128 - ROPE_DIM))).reshape(Q_LORA, N_HEADS * 128)
    w_pe, w_pe_sw = bf(pad_pe(uq[:, :, NOPE_DIM:])), bf(pad_pe(_swap_halves(uq[:, :, NOPE_DIM:])))
    w_ukt = bf(jnp.transpose(b_w_uk[0], (1, 2, 0)))
    w_uv = bf(jnp.transpose(b_w_uv[0], (1, 0, 2)))
    zp, zs = _rms_proj(hp, ln_gain[1], w_in), _rms_proj(hs, ln_gain[1], w_in)
    cos_p, sin_p = _rope_tables(jnp.arange(t))
    cos_s, sin_s = _rope_tables(n_past + jnp.arange(n_new))
    tmq = _pick(mp, (256, 128, 64, 32, 16, 8))
    assert t % tmq == 0
    ql_p, qp_p, lat_p, kpe_p, latb_p, kpeb_p, latt_p = _mla_q(zp, cos_p, sin_p, b_q_norm[0], b_kv_norm[0],
                                                               w_nope, w_pe, w_pe_sw, w_ukt, t // tmq)
    ql_s, qp_s, lat_s, kpe_s, _, _, _ = _mla_q(zs, jnp.tile(cos_s, (ns, 1)), jnp.tile(sin_s, (ns, 1)), b_q_norm[0],
                                                b_kv_norm[0], w_nope, w_pe, w_pe_sw, w_ukt, ms)
    o_p = _mla_prompt(ql_p, qp_p, latb_p, latt_p, kpeb_p, w_uv, nb, t)
    to_rows = lambda a, w_: jnp.transpose(a.reshape(ns, n_new, N_HEADS, w_), (0, 2, 1, 3)).reshape(ns, N_HEADS * n_new, w_)
    qls = to_rows(ql_s, KV_LORA)
    qps = to_rows(qp_s.reshape(ms, N_HEADS, 128)[:, :, :ROPE_DIM].reshape(ms, N_HEADS * ROPE_DIM), ROPE_DIM)
    padn = lambda a: jnp.pad(a.reshape(ns, n_new, -1), ((0, 0), (0, n_pad - n_new), (0, 0)))
    ii = jnp.arange(N_HEADS * n_new)[:, None] % n_new
    mask_new = jnp.where(jnp.arange(n_pad)[None, :] <= ii, 0.0, NEG).astype(F32)
    ol_s = _mla_sample(page_table, qls, qps, cache_b_lat[0], jnp.swapaxes(cache_b_kpe[0], -1, -2),
                       padn(lat_s), padn(kpe_s), mask_new)
    ol_s = jnp.transpose(ol_s.reshape(ns, N_HEADS, n_new, KV_LORA), (0, 2, 1, 3)).reshape(ms, N_HEADS * KV_LORA)
    hp = _gated_out(hp, zp, [o_p], [], w_out)
    hs = _gated_out(hs, zs, [_mla_o(ol_s, w_uv)], [], w_out)
    b_lat_prompt, b_lat_sample = lat_p.reshape(1, nb, t, KV_LORA), lat_s.reshape(1, ns, n_new, KV_LORA)
    b_kpe_prompt, b_kpe_sample = kpe_p.reshape(1, nb, t, ROPE_DIM), kpe_s.reshape(1, ns, n_new, ROPE_DIM)

    n_grp = len(C_WINDOWS)
    gw = BRANCH + 2 * KV_C * HEAD_DIM
    w_in = bf(_gate_first(c_w_in[0], n_grp * gw))
    w_out = bf(c_w_out[0])
    zp, zs = _rms_proj(hp, ln_gain[2], w_in), _rms_proj(hs, ln_gain[2], w_in)
    outs_p, lses_p, outs_s, lses_s, c_prompt, c_sample = [], [], [], [], [], []
    for gi, (win, dil, buf) in enumerate(zip(C_WINDOWS, C_DILATIONS, (cache_c_kv1, cache_c_kv2, cache_c_kv3))):
        c0 = BRANCH + gi * gw
        kv_p = zp[:, c0 + BRANCH:c0 + gw]
        c_prompt.append(kv_p.reshape(nb, t, 2, KV_C, HEAD_DIM)[:, -min(win, t):][None])
        assert win // dil == BLOCK
        bias = band_bias[dil]
        cols = (c0, c0 + BRANCH, c0 + BRANCH + KV_C * HEAD_DIM)
        if dil == 1:
            o, lse = _band_attention(zp, *cols, nb, t, KV_C, bias, None, True)
        else:
            o, lse = _dilated_band_attention(zp, *cols, nb, t, KV_C, dil, bias)
        outs_p.append(o)
        lses_p.append(lse)
        n_buf = buf.shape[2]
        new_flat, k_new, v_new = _new_kv(zs[:, c0 + BRANCH:c0 + gw], ns, n_new, KV_C, n_pad)
        bias_buf, bias_new = _decode_bias(f_rev, n_buf, n_new, win, dil, KV_C, n_pad)
        o, lse, cache = _window_decode(_heads_to_rows(zs[:, c0:c0 + BRANCH], ns, n_new, KV_C),
                                       buf[0].reshape(-1, HEAD_DIM), new_flat, k_new, v_new, bias_buf, bias_new,
                                       None, KV_C, n_buf, n_new, True)
        outs_s.append(_rows_to_heads(o, ns, n_new, KV_C))
        lses_s.append(_rows_to_heads(lse, ns, n_new, KV_C))
        c_sample.append(cache.reshape(buf.shape))
    hp = _gated_out(hp, zp, outs_p, lses_p, w_out)
    hs = _gated_out(hs, zs, outs_s, lses_s, w_out)

    nkv = 2 * KV_D * HEAD_DIM
    w_in = bf(_gate_first(d_w_in[0], BRANCH + nkv))
    w_out = bf(d_w_out[0])
    zp, zs = _rms_proj(hp, ln_gain[3], w_in), _rms_proj(hs, ln_gain[3], w_in)
    lam_p, subln = d_lambda[0].astype(F32), d_subln[0].reshape(1, HEAD_DIM).astype(F32)
    tq = _pick(t, (256, 128))
    o_p = _diff_prompt(zp, BRANCH, 2 * BRANCH, 2 * BRANCH + KV_D * HEAD_DIM, nb, t,
                       _causal_bias_rows(f[:, :t], tq).reshape(t // tq, KV_D, N_HEADS // KV_D, 1, 2 * tq), lam_p, subln, tq)
    d_kv_prompt = zp[:, 2 * BRANCH:].reshape(1, nb, t, 2, KV_D, HEAD_DIM)
    d_kv_sample = zs[:, 2 * BRANCH:].reshape(1, ns, n_new, 2, KV_D, HEAD_DIM)
    _, k_new, v_new = _new_kv(zs[:, 2 * BRANCH:], ns, n_new, KV_D, n_pad)
    bias_past, bias_new = _decode_bias(f_rev, n_past, n_new, n_past + n_new, 1, KV_D, n_pad)
    q = _heads_to_rows(zs[:, BRANCH:2 * BRANCH], ns, n_new, KV_D)
    q = jnp.concatenate(_split_components(q), axis=2)
    o_s = _diff_sample(page_table, q, cache_d_kv[0].reshape(-1, HEAD_DIM), k_new, v_new, bias_past, bias_new, lam_p, subln)
    y_prompt = _gated_out(hp, zp, [o_p], [], w_out, final_gain).reshape(nb, t, d)
    y_sample = _gated_out(hs, zs, [_rows_to_heads(o_s, ns, n_new, KV_D)], [], w_out, final_gain).reshape(ns, n_new, d)
    return (y_prompt, y_sample, a_kv_prompt, a_kv_sample, b_lat_prompt, b_lat_sample, b_kpe_prompt, b_kpe_sample,
            c_prompt[0], c_sample[0], c_prompt[1], c_sample[1], c_prompt[2], c_sample[2], d_kv_prompt, d_kv_sample)
```

```python
import functools
import math

import jax
import jax.numpy as jnp
from jax import lax
from jax.experimental import pallas as pl
from jax.experimental.pallas import tpu as pltpu

F32 = jnp.float32
BF16 = jnp.bfloat16
NEG = -1e30
EPS = 1e-6

D_MODEL = 2048
HEAD_DIM = 128
N_HEADS = 16
BRANCH = N_HEADS * HEAD_DIM
PAGE = 128
BLOCK = 128
N_BUCKETS = 32
T5_MAX_DISTANCE = 2048
WIN_A, KV_A = 128, 2
Q_LORA, KV_LORA, NOPE_DIM, ROPE_DIM = 512, 512, 128, 64
ROPE_THETA = 10000.0
C_WINDOWS, C_DILATIONS, KV_C = (128, 512, 2048), (1, 4, 16), 4
DH_D, KV_D = 64, 4
D_LAYER = 3
PAGES_PER_STEP = 16
VMEM_LIMIT_BYTES = 56 * 1024 * 1024


def _cparams(*sem):
    return pltpu.CompilerParams(dimension_semantics=sem, vmem_limit_bytes=VMEM_LIMIT_BYTES)


def _pick(n, cands):
    for c in cands:
        if n % c == 0:
            return c
    raise ValueError(f"no tile for {n}")


_ROW_TILES = (512, 256, 128, 64, 32, 16, 8)


def _dot_t(a, b):
    return lax.dot_general(a, b, (((1,), (1,)), ((), ())), preferred_element_type=F32)


def _dot(a, b):
    return jnp.dot(a, b, preferred_element_type=F32)


def _rms_proj_kernel(x_ref, g_ref, w_ref, o_ref, xn_ref):
    @pl.when(pl.program_id(1) == 0)
    def _():
        x = x_ref[...]
        y = x * lax.rsqrt(jnp.mean(x * x, axis=-1, keepdims=True) + EPS) * g_ref[...]
        xn_ref[...] = y.astype(BF16)

    o_ref[...] = _dot(xn_ref[...], w_ref[...])


def _rms_proj(x, g, w):
    m, d = x.shape
    n = w.shape[1]
    tm = _pick(m, (1024,) + _ROW_TILES)
    tn = _pick(n, (1024, 768, 640, 512, 384, 256, 128))
    return pl.pallas_call(
        _rms_proj_kernel,
        grid=(m // tm, n // tn),
        in_specs=[pl.BlockSpec((tm, d), lambda i, j: (i, 0)),
                  pl.BlockSpec((1, d), lambda i, j: (0, 0)),
                  pl.BlockSpec((d, tn), lambda i, j: (0, j))],
        out_specs=pl.BlockSpec((tm, tn), lambda i, j: (i, j)),
        out_shape=jax.ShapeDtypeStruct((m, n), F32),
        scratch_shapes=[pltpu.VMEM((tm, d), BF16)],
        compiler_params=_cparams("parallel", "arbitrary"),
        name="rms_proj",
    )(x, g.reshape(1, d), w)


def _gated_out_kernel(*refs, n_groups, final_norm):
    h_ref, gate_ref, w_ref = refs[:3]
    o_ref = refs[-1]
    parts = refs[4:-1] if final_norm else refs[3:-1]
    if n_groups == 1:
        o = parts[0][...].astype(F32)
    else:
        outs = [r[...] for r in parts[:n_groups]]
        lses = [r[...] for r in parts[n_groups:]]
        m = functools.reduce(jnp.maximum, lses)
        es = [jnp.exp(l - m) for l in lses]
        den = functools.reduce(lambda a, b: a + b, es)
        o = functools.reduce(lambda a, b: a + b, [e * x for e, x in zip(es, outs)]) / den
    gate = gate_ref[...]
    a = (o * (gate * jax.nn.sigmoid(gate))).astype(BF16)
    y = h_ref[...] + _dot(a, w_ref[...])
    if final_norm:
        y = y * lax.rsqrt(jnp.mean(y * y, axis=-1, keepdims=True) + EPS) * refs[3][...]
    o_ref[...] = y


def _gated_out(h, z, outs, lses, w_out, final_gain=None):
    m, d = h.shape
    n_groups = len(outs)
    tm = _pick(m, _ROW_TILES if n_groups == 1 else _ROW_TILES[1:])
    parts = list(outs) + (list(lses) if n_groups > 1 else [])
    row = pl.BlockSpec((tm, BRANCH), lambda i: (i, 0))
    final = [] if final_gain is None else [final_gain.reshape(1, d)]
    return pl.pallas_call(
        functools.partial(_gated_out_kernel, n_groups=n_groups, final_norm=final_gain is not None),
        grid=(m // tm,),
        in_specs=[pl.BlockSpec((tm, d), lambda i: (i, 0)), row,
                  pl.BlockSpec((BRANCH, d), lambda i: (0, 0), pipeline_mode=pl.Buffered(1))]
        + [pl.BlockSpec((1, d), lambda i: (0, 0))] * len(final) + [row] * len(parts),
        out_specs=pl.BlockSpec((tm, d), lambda i: (i, 0)),
        out_shape=jax.ShapeDtypeStruct((m, d), F32),
        compiler_params=_cparams("parallel"),
        name="gated_out",
    )(h, z, w_out, *final, *parts)


def _t5_table(rel_bias, n_dist):
    exact = N_BUCKETS // 2
    dist = jnp.arange(n_dist)
    df = jnp.maximum(dist, 1).astype(F32)
    far = exact + (jnp.log(df / exact) / math.log(T5_MAX_DISTANCE / exact) * (N_BUCKETS - exact)).astype(jnp.int32)
    bucket = jnp.where(dist < exact, dist, jnp.minimum(far, N_BUCKETS - 1))
    return rel_bias[bucket].astype(F32).T, rel_bias[bucket[::-1]].astype(F32).T


def _toeplitz(u, rows, cols):
    p = u.shape[-1]
    lead = u.shape[:-1]
    assert p >= rows + cols
    flat = jnp.tile(u, (1,) * len(lead) + (rows,))[..., :rows * (p - 1)]
    return flat.reshape(lead + (rows, p - 1))[..., :cols]


def _neg(*shape):
    return jnp.full(shape, NEG, F32)


def _band_bias(f_rev, dil):
    n = f_rev.shape[1]
    val_rev = f_rev[:, n - 1 - BLOCK * dil::dil][:, :BLOCK + 1]
    u = jnp.concatenate([val_rev, _neg(N_HEADS, 3 * BLOCK - 1)], axis=1)
    return _toeplitz(u, BLOCK, 2 * BLOCK)


def _band_bias_t(f, dil):
    far = f[:, BLOCK * dil:BLOCK * dil + 1]
    u = jnp.concatenate([far, _neg(N_HEADS, 3 * BLOCK - 1), f[:, :BLOCK * dil:dil]], axis=1)
    return _toeplitz(u, 2 * BLOCK, BLOCK)


def _causal_bias_rows(f, tq, tk):
    n_delta = f.shape[1] // tq
    fp = jnp.concatenate([_neg(N_HEADS, tk), f, _neg(N_HEADS, tq)], axis=1)
    return jnp.stack([jnp.concatenate([fp[:, tk + d * tq:tk + (d + 1) * tq], fp[:, d * tq:d * tq + tk]], axis=1)
                      for d in range(n_delta)])


def _decode_bias(f_rev, n_past, n_new, window, dil, n_kv, n_pad):
    g = N_HEADS // n_kv
    n = n_past + n_new + 1
    dd = n - 1 - jnp.arange(n)
    rev = jnp.where(((dd <= window) & (dd % dil == 0))[None, :], f_rev[:, f_rev.shape[1] - n:], NEG)
    u = jnp.concatenate([rev[:, n_new:], _neg(N_HEADS, n_pad), rev[:, :n_new]], axis=1)
    t = _toeplitz(u, n_new, n_past + n_pad).reshape(n_kv, g * n_new, n_past + n_pad)
    return t[..., :n_past], t[..., n_past:]


def _band_kernel(*refs, n_kv, has_sink, want_lse, scale):
    q_refs, (kp_ref, kc_ref, vp_ref, vc_ref, bias_ref) = refs[:2], refs[2:7]
    rest = refs[7:]
    if has_sink:
        sink_ref, rest = rest[0], rest[1:]
    o_ref = rest[0]
    g = N_HEADS // n_kv
    half = N_HEADS // 2

    def q_head(head):
        c = (head % half) * HEAD_DIM
        return q_refs[head // half][:, c:c + HEAD_DIM]

    first = pl.program_id(1) == 0
    for h in range(n_kv):
        cols = slice(h * HEAD_DIM, (h + 1) * HEAD_DIM)
        qs = jnp.concatenate([q_head(h * g + i) for i in range(g)], axis=0)
        qs = (qs * scale).astype(BF16)
        k = jnp.concatenate([kp_ref[:, cols], kc_ref[:, cols]], axis=0).astype(BF16)
        v = jnp.concatenate([vp_ref[:, cols], vc_ref[:, cols]], axis=0)
        if want_lse:
            s = _dot_t(qs, k) + jnp.concatenate([bias_ref[h * g + i] for i in range(g)], axis=0)
            col = lax.broadcasted_iota(jnp.int32, s.shape, 1)
            s = jnp.where(first & (col < BLOCK), NEG, s)
            m = jnp.max(s, axis=-1, keepdims=True)
            p = jnp.exp(s - m)
            l = jnp.sum(p, axis=-1, keepdims=True)
            o = _dot(p.astype(BF16), v.astype(BF16)) / l
            lse = jnp.broadcast_to(m + jnp.log(l), o.shape)
            for i in range(g):
                c = (h * g + i) * HEAD_DIM
                o_ref[:, c:c + HEAD_DIM] = o[i * BLOCK:(i + 1) * BLOCK].astype(o_ref.dtype)
                rest[1][:, c:c + HEAD_DIM] = lse[i * BLOCK:(i + 1) * BLOCK]
            continue
        st = _dot_t(k, qs) + jnp.concatenate([bias_ref[h * g + i] for i in range(g)], axis=1)
        key = lax.broadcasted_iota(jnp.int32, st.shape, 0)
        st = jnp.where(first & (key < BLOCK), NEG, st)
        m = jnp.max(st, axis=0, keepdims=True)
        if has_sink:
            m = jnp.maximum(m, sink_ref[h])
        p = jnp.exp(st - m)
        l = jnp.sum(p, axis=0, keepdims=True)
        if has_sink:
            l = l + jnp.exp(sink_ref[h] - m)
        o = (_dot(v.T.astype(BF16), p.astype(BF16)) / l).T
        for i in range(g):
            c = (h * g + i) * HEAD_DIM
            o_ref[:, c:c + HEAD_DIM] = o[i * BLOCK:(i + 1) * BLOCK].astype(o_ref.dtype)


def _band_attention(src, q_col, k_col, v_col, n_seq, seq_len, n_kv, bias, sink, want_lse):
    g = N_HEADS // n_kv
    nb = seq_len // BLOCK
    qw, kw = BRANCH // 2, n_kv * HEAD_DIM
    qc, kc, vc = q_col // qw, k_col // kw, v_col // kw
    cur = lambda w, c0: pl.BlockSpec((BLOCK, w), lambda b, j: (b * nb + j, c0))
    prev = lambda w, c0: pl.BlockSpec((BLOCK, w), lambda b, j: (b * nb + jnp.maximum(j - 1, 0), c0))
    full = lambda a: pl.BlockSpec(a.shape, lambda b, j: (0,) * a.ndim)
    in_specs = [cur(qw, qc), cur(qw, qc + 1), prev(kw, kc), cur(kw, kc), prev(kw, vc), cur(kw, vc), full(bias)]
    args = [src, src, src, src, src, src, bias]
    if sink is not None:
        in_specs.append(full(sink))
        args.append(sink)
    out_spec = pl.BlockSpec((BLOCK, BRANCH), lambda b, j: (b * nb + j, 0))
    out_sds = jax.ShapeDtypeStruct((n_seq * seq_len, BRANCH), F32 if want_lse else BF16)
    return pl.pallas_call(
        functools.partial(_band_kernel, n_kv=n_kv, has_sink=sink is not None, want_lse=want_lse, scale=HEAD_DIM ** -0.5),
        grid=(n_seq, nb),
        in_specs=in_specs,
        out_specs=[out_spec, out_spec] if want_lse else out_spec,
        out_shape=[out_sds, out_sds] if want_lse else out_sds,
        compiler_params=_cparams("parallel", "arbitrary"),
        name="band_attention",
    )(*args)


def _dilated_band_kernel(q_ref, k_ref, v_ref, bias_ref, o_ref, lse_ref, *, dil, n_span, scale):
    bias = bias_ref[0]
    for j in range(n_span):
        for ph in range(dil):
            rows = pl.ds(j * dil * BLOCK + ph, BLOCK, stride=dil)
            q = (q_ref[rows, :] * scale).astype(BF16)
            prev = pl.ds(max(j - 1, 0) * dil * BLOCK + ph, BLOCK, stride=dil)
            k = jnp.concatenate([k_ref[prev, :], k_ref[rows, :]], axis=0)
            v = jnp.concatenate([v_ref[prev, :], v_ref[rows, :]], axis=0)
            s = _dot_t(q, k.astype(BF16)) + bias
            if j == 0:
                s = jnp.where(lax.broadcasted_iota(jnp.int32, s.shape, 1) < BLOCK, NEG, s)
            m = jnp.max(s, axis=-1, keepdims=True)
            p = jnp.exp(s - m)
            l = jnp.sum(p, axis=-1, keepdims=True)
            o = _dot(p.astype(BF16), v.astype(BF16)) / l
            o_ref[rows, :] = o
            lse_ref[rows, :] = jnp.broadcast_to(m + jnp.log(l), o.shape)


def _dilated_band_attention(src, q_col, k_col, v_col, n_seq, seq_len, n_kv, dil, bias):
    g = N_HEADS // n_kv
    n_span = seq_len // (dil * BLOCK)
    qc, kc, vc = q_col // HEAD_DIM, k_col // HEAD_DIM, v_col // HEAD_DIM
    seq = lambda c0, per: pl.BlockSpec((seq_len, HEAD_DIM), lambda b, h: (b, c0 + h // per))
    out_spec = seq(0, 1)
    out_sds = jax.ShapeDtypeStruct((n_seq * seq_len, BRANCH), F32)
    return pl.pallas_call(
        functools.partial(_dilated_band_kernel, dil=dil, n_span=n_span, scale=HEAD_DIM ** -0.5),
        grid=(n_seq, N_HEADS),
        in_specs=[seq(qc, 1), seq(kc, g), seq(vc, g),
                  pl.BlockSpec((1, BLOCK, 2 * BLOCK), lambda b, h: (h, 0, 0))],
        out_specs=[out_spec, out_spec],
        out_shape=[out_sds, out_sds],
        compiler_params=_cparams("parallel", "arbitrary"),
        name="dilated_band_attention",
    )(src, src, src, bias)


def _sink_lanes(sink, n_kv, rows_per_head):
    g = N_HEADS // n_kv
    return jnp.broadcast_to(sink.astype(F32).reshape(n_kv, g, 1), (n_kv, g, rows_per_head)).reshape(n_kv, 1, g * rows_per_head)


def _sink_rows(sink, n_kv, rows_per_head):
    g = N_HEADS // n_kv
    return jnp.broadcast_to(sink.astype(F32).reshape(n_kv, g, 1), (n_kv, g, rows_per_head)).reshape(n_kv, g * rows_per_head, 1)


def _window_decode_kernel(*refs, n_kv, n_past, n_new, sps, has_sink, want_lse, scale):
    q_ref, buf_ref, new_ref, kn_ref, vn_ref, bb_ref, bn_ref = refs[:7]
    rest = refs[7:]
    if has_sink:
        sink_ref, rest = rest[0], rest[1:]
    o_ref, cache_ref = rest[0], rest[-1]
    rw = 2 * n_kv
    keep = (n_past - n_new) * rw
    for s in range(sps):
        base = s * n_past * rw
        cache_ref[pl.ds(base, keep), :] = buf_ref[pl.ds(base + n_new * rw, keep), :]
        cache_ref[pl.ds(base + keep, n_new * rw), :] = new_ref[pl.ds(s * n_new * rw, n_new * rw), :]
        for h in range(n_kv):
            kb = buf_ref[pl.ds(base + h, n_past, stride=rw), :].astype(BF16)
            vb = buf_ref[pl.ds(base + n_kv + h, n_past, stride=rw), :].astype(BF16)
            q = (q_ref[s, h] * scale).astype(BF16)
            s1 = _dot_t(q, kb) + bb_ref[h]
            s2 = _dot_t(q, kn_ref[s, h].astype(BF16)) + bn_ref[h]
            m = jnp.maximum(jnp.max(s1, axis=-1, keepdims=True), jnp.max(s2, axis=-1, keepdims=True))
            if has_sink:
                m = jnp.maximum(m, sink_ref[h])
            p1 = jnp.exp(s1 - m)
            p2 = jnp.exp(s2 - m)
            l = jnp.sum(p1, axis=-1, keepdims=True) + jnp.sum(p2, axis=-1, keepdims=True)
            if has_sink:
                l = l + jnp.exp(sink_ref[h] - m)
            o = (_dot(p1.astype(BF16), vb) + _dot(p2.astype(BF16), vn_ref[s, h].astype(BF16))) / l
            o_ref[s, h] = o
            if want_lse:
                rest[1][s, h] = jnp.broadcast_to(m + jnp.log(l), o.shape)


def _window_decode(q, buf_flat, new_flat, k_new, v_new, bias_buf, bias_new, sink, n_kv, n_past, n_new, want_lse):
    n_seq, _, rows, dh = q.shape
    rw = 2 * n_kv
    n_pad = k_new.shape[2]
    sps = _pick(n_seq, [c for c in (8, 4, 2, 1) if c * n_past <= 1024 or c == 1])
    full = lambda shape: pl.BlockSpec(shape, lambda b: (0,) * len(shape))
    per_seq = lambda shape: pl.BlockSpec(shape, lambda b: (b,) + (0,) * (len(shape) - 1))
    in_specs = [per_seq((sps, n_kv, rows, dh)), per_seq((sps * n_past * rw, dh)), per_seq((sps * n_new * rw, dh)),
                per_seq((sps, n_kv, n_pad, dh)), per_seq((sps, n_kv, n_pad, dh)),
                full((n_kv, rows, n_past)), full((n_kv, rows, n_pad))]
    args = [q, buf_flat, new_flat, k_new, v_new, bias_buf, bias_new]
    if sink is not None:
        in_specs.append(full((n_kv, rows, 1)))
        args.append(sink)
    o_spec, o_sds = per_seq((sps, n_kv, rows, dh)), jax.ShapeDtypeStruct(q.shape, F32)
    out_specs = [o_spec] + ([o_spec] if want_lse else []) + [per_seq((sps * n_past * rw, dh))]
    out_shape = [o_sds] + ([o_sds] if want_lse else []) + [jax.ShapeDtypeStruct(buf_flat.shape, F32)]
    return pl.pallas_call(
        functools.partial(_window_decode_kernel, n_kv=n_kv, n_past=n_past, n_new=n_new, sps=sps,
                          has_sink=sink is not None, want_lse=want_lse, scale=dh ** -0.5),
        grid=(n_seq // sps,),
        in_specs=in_specs,
        out_specs=out_specs,
        out_shape=out_shape,
        compiler_params=_cparams("parallel"),
        name="window_decode",
    )(*args)


def _mla_q_kernel(cq_ref, ckv_ref, kpe_ref, cos_ref, sin_ref, qn_ref, kvn_ref, wn_ref, wp_ref, wps_ref, wuk_ref,
                  ql_ref, qp_ref, lat_ref, kpo_ref, latb_ref, kpb_ref, latt_ref, *, scale):
    def rms(x, g):
        return x * lax.rsqrt(jnp.mean(x * x, axis=-1, keepdims=True) + EPS) * g

    cq = rms(cq_ref[...], qn_ref[...]).astype(BF16)
    lat = rms(ckv_ref[...], kvn_ref[...])
    lat_ref[...] = lat
    latb_ref[...] = lat.astype(BF16)
    latt_ref[...] = lat.T.astype(BF16)
    cos, sin = cos_ref[...], sin_ref[...]
    kp2 = kpe_ref[...]
    kp = kp2 * cos + pltpu.roll(kp2, ROPE_DIM, 1) * sin
    kpo_ref[...] = kp[:, :ROPE_DIM]
    lane = lax.broadcasted_iota(jnp.int32, kp.shape, 1)
    kpb_ref[...] = jnp.where(lane < ROPE_DIM, kp, 0.0).astype(BF16)
    n_rep = wp_ref.shape[1] // cos.shape[1]
    qp = _dot(cq, wp_ref[...]) * jnp.tile(cos, (1, n_rep)) + _dot(cq, wps_ref[...]) * jnp.tile(sin, (1, n_rep))
    qp_ref[...] = (qp * scale).astype(BF16)
    qn = _dot(cq, wn_ref[...])
    for h in range(N_HEADS):
        qh = qn[:, h * NOPE_DIM:(h + 1) * NOPE_DIM].astype(BF16)
        ql_ref[:, h * KV_LORA:(h + 1) * KV_LORA] = (_dot(qh, wuk_ref[h]) * scale).astype(BF16)


def _mla_q(z, cos, sin, q_norm, kv_norm, w_nope, w_pe, w_pe_sw, w_ukt, pos_tiles):
    m = z.shape[0]
    tm = _pick(m, (256, 128, 64, 32, 16, 8))
    c0 = BRANCH // Q_LORA
    full = lambda a: pl.BlockSpec(a.shape, lambda i: (0,) * a.ndim)
    row = lambda w: pl.BlockSpec((tm, w), lambda i: (i, 0))
    pos = pl.BlockSpec((tm, 128), lambda i: (i % pos_tiles, 0))
    qn, kvn = q_norm.reshape(1, Q_LORA), kv_norm.reshape(1, KV_LORA)
    return pl.pallas_call(
        functools.partial(_mla_q_kernel, scale=(NOPE_DIM + ROPE_DIM) ** -0.5),
        grid=(m // tm,),
        in_specs=[pl.BlockSpec((tm, Q_LORA), lambda i: (i, c0)),
                  pl.BlockSpec((tm, KV_LORA), lambda i: (i, c0 + 1)),
                  pl.BlockSpec((tm, 128), lambda i: (i, (BRANCH + Q_LORA + KV_LORA) // 128)),
                  pos, pos, full(qn), full(kvn), full(w_nope), full(w_pe), full(w_pe_sw), full(w_ukt)],
        out_specs=[row(N_HEADS * KV_LORA), row(N_HEADS * 128), row(KV_LORA), row(ROPE_DIM), row(KV_LORA), row(128),
                   pl.BlockSpec((KV_LORA, tm), lambda i: (0, i))],
        out_shape=[jax.ShapeDtypeStruct((m, N_HEADS * KV_LORA), BF16),
                   jax.ShapeDtypeStruct((m, N_HEADS * 128), BF16),
                   jax.ShapeDtypeStruct((m, KV_LORA), F32),
                   jax.ShapeDtypeStruct((m, ROPE_DIM), F32),
                   jax.ShapeDtypeStruct((m, KV_LORA), BF16),
                   jax.ShapeDtypeStruct((m, 128), BF16),
                   jax.ShapeDtypeStruct((KV_LORA, m), BF16)],
        compiler_params=_cparams("parallel"),
        name="mla_q",
    )(z, z, z, cos, sin, qn, kvn, w_nope, w_pe, w_pe_sw, w_ukt)


def _mla_prompt_kernel(qi_ref, ki_ref, ql_ref, qp_ref, c_ref, ct_ref, kp_ref, wuv_ref, o_ref,
                       qs_ref, qps_ref, m_ref, l_ref, acc_ref, *, tq, tk):
    qi, ki = qi_ref[pl.program_id(1)], ki_ref[pl.program_id(1)]
    last = (qi * tq + tq - 1) // tk

    @pl.when(ki == 0)
    def _():
        for h in range(N_HEADS):
            qs_ref[h * tq:(h + 1) * tq, :] = ql_ref[:, h * KV_LORA:(h + 1) * KV_LORA]
            qps_ref[h * tq:(h + 1) * tq, :] = qp_ref[:, h * 128:(h + 1) * 128]
        m_ref[...] = jnp.full(m_ref.shape, NEG, F32)
        l_ref[...] = jnp.zeros(l_ref.shape, F32)
        acc_ref[...] = jnp.zeros(acc_ref.shape, F32)

    def step(masked):
        st = _dot_t(c_ref[...], qs_ref[...]) + _dot_t(kp_ref[...], qps_ref[...])
        if masked:
            kpos = ki * tk + lax.broadcasted_iota(jnp.int32, st.shape, 0)
            qpos = qi * tq + (lax.broadcasted_iota(jnp.int32, st.shape, 1) & (tq - 1))
            st = jnp.where(kpos <= qpos, st, NEG)
        m_old = m_ref[...]
        m_new = jnp.maximum(m_old, jnp.max(st, axis=0, keepdims=True))
        alpha = jnp.exp(m_old - m_new)
        p = jnp.exp(st - m_new)
        l_ref[...] = alpha * l_ref[...] + jnp.sum(p, axis=0, keepdims=True)
        acc_ref[...] = alpha * acc_ref[...] + _dot(ct_ref[...], p.astype(BF16))
        m_ref[...] = m_new

    @pl.when(ki < last)
    def _():
        step(False)

    @pl.when(ki == last)
    def _():
        step(True)
        ot = acc_ref[...] / l_ref[...]
        for h in range(N_HEADS):
            oh = ot[:, h * tq:(h + 1) * tq].T.astype(BF16)
            o_ref[:, h * HEAD_DIM:(h + 1) * HEAD_DIM] = _dot(oh, wuv_ref[h]).astype(o_ref.dtype)


def _mla_prompt(q_lat, q_pe, lat_b, lat_t, kpe_b, w_uv, n_seq, seq_len):
    tq = BLOCK
    tk = _pick(seq_len, (512, 256, 128))
    nq, nk = seq_len // tq, seq_len // tk
    pairs = [(i, j) for i in range(nq) for j in range((i * tq + tq - 1) // tk + 1)]
    qi_tab = jnp.asarray([p[0] for p in pairs], jnp.int32)
    ki_tab = jnp.asarray([p[1] for p in pairs], jnp.int32)
    rows = N_HEADS * tq
    qrow = lambda w: pl.BlockSpec((tq, w), lambda b, p, qt, kt: (b * nq + qt[p], 0))
    krow = lambda w: pl.BlockSpec((tk, w), lambda b, p, qt, kt: (b * nk + kt[p], 0))
    grid_spec = pltpu.PrefetchScalarGridSpec(
        num_scalar_prefetch=2,
        grid=(n_seq, len(pairs)),
        in_specs=[qrow(N_HEADS * KV_LORA), qrow(N_HEADS * 128), krow(KV_LORA),
                  pl.BlockSpec((KV_LORA, tk), lambda b, p, qt, kt: (0, b * nk + kt[p])),
                  krow(128),
                  pl.BlockSpec(w_uv.shape, lambda b, p, qt, kt: (0, 0, 0))],
        out_specs=qrow(BRANCH),
        scratch_shapes=[pltpu.VMEM((rows, KV_LORA), BF16), pltpu.VMEM((rows, 128), BF16),
                        pltpu.VMEM((1, rows), F32), pltpu.VMEM((1, rows), F32),
                        pltpu.VMEM((KV_LORA, rows), F32)],
    )
    return pl.pallas_call(
        functools.partial(_mla_prompt_kernel, tq=tq, tk=tk),
        grid_spec=grid_spec,
        out_shape=jax.ShapeDtypeStruct((n_seq * seq_len, BRANCH), BF16),
        compiler_params=_cparams("parallel", "arbitrary"),
        name="mla_prompt",
    )(qi_tab, ki_tab, q_lat, q_pe, lat_b, lat_t, kpe_b, w_uv)


def _mla_sample_kernel(pt_ref, ql_ref, qp_ref, *refs, pps):
    lat_refs, kpt_refs = refs[:pps], refs[pps:2 * pps]
    cn_ref, kn_ref, mask_ref, o_ref, m_ref, l_ref, acc_ref = refs[2 * pps:]
    p_idx = pl.program_id(1)

    @pl.when(p_idx == 0)
    def _():
        m_ref[...] = jnp.full(m_ref.shape, NEG, F32)
        l_ref[...] = jnp.zeros(l_ref.shape, F32)
        acc_ref[...] = jnp.zeros(acc_ref.shape, F32)

    ql, qp = ql_ref[0], qp_ref[0]

    def update(s, v):
        m_old = m_ref[...]
        m_new = jnp.maximum(m_old, jnp.max(s, axis=-1, keepdims=True))
        alpha = jnp.exp(m_old - m_new)
        p = jnp.exp(s - m_new)
        l_ref[...] = alpha * l_ref[...] + jnp.sum(p, axis=-1, keepdims=True)
        acc_ref[...] = alpha * acc_ref[...] + _dot(p.astype(BF16), v)
        m_ref[...] = m_new

    c = jnp.concatenate([r[0] for r in lat_refs], axis=0).astype(BF16)
    kpt = jnp.concatenate([r[0] for r in kpt_refs], axis=1).astype(BF16)
    update(_dot_t(ql, c) + _dot(qp, kpt), c)

    @pl.when(p_idx == pl.num_programs(1) - 1)
    def _():
        cn = cn_ref[0].astype(BF16)
        update(_dot_t(ql, cn) + _dot_t(qp, kn_ref[0].astype(BF16)) + mask_ref[...], cn)
        o_ref[0] = (acc_ref[...] / l_ref[...]).astype(BF16)


def _mla_sample(page_table, q_lat, q_pe, lat_pool, kpet_pool, c_new, k_new, mask_new):
    n_seq, rows, _ = q_lat.shape
    n_pages = page_table.shape[1]
    pps = _pick(n_pages, (PAGES_PER_STEP, 4, 2, 1))
    n_pad = c_new.shape[1]

    def page(i, shape):
        return pl.BlockSpec(shape, lambda b, p, pt: (pt[b, p * pps + i], 0, 0))

    per_seq = lambda shape: pl.BlockSpec(shape, lambda b, p, pt: (b,) + (0,) * (len(shape) - 1))
    grid_spec = pltpu.PrefetchScalarGridSpec(
        num_scalar_prefetch=1,
        grid=(n_seq, n_pages // pps),
        in_specs=[per_seq((1, rows, KV_LORA)), per_seq((1, rows, ROPE_DIM))]
        + [page(i, (1, PAGE, KV_LORA)) for i in range(pps)]
        + [page(i, (1, ROPE_DIM, PAGE)) for i in range(pps)]
        + [per_seq((1, n_pad, KV_LORA)), per_seq((1, n_pad, ROPE_DIM)),
           pl.BlockSpec((rows, n_pad), lambda b, p, pt: (0, 0))],
        out_specs=per_seq((1, rows, KV_LORA)),
        scratch_shapes=[pltpu.VMEM((rows, 1), F32), pltpu.VMEM((rows, 1), F32), pltpu.VMEM((rows, KV_LORA), F32)],
    )
    return pl.pallas_call(
        functools.partial(_mla_sample_kernel, pps=pps),
        grid_spec=grid_spec,
        out_shape=jax.ShapeDtypeStruct((n_seq, rows, KV_LORA), BF16),
        compiler_params=_cparams("parallel", "arbitrary"),
        name="mla_sample",
    )(page_table, q_lat, q_pe, *([lat_pool] * pps), *([kpet_pool] * pps), c_new, k_new, mask_new)


def _mla_o_kernel(ol_ref, w_ref, o_ref):
    for h in range(N_HEADS):
        o_ref[:, h * HEAD_DIM:(h + 1) * HEAD_DIM] = _dot(ol_ref[:, h * KV_LORA:(h + 1) * KV_LORA], w_ref[h])


def _mla_o(o_lat, w_uv):
    m = o_lat.shape[0]
    tm = _pick(m, (256, 128, 64, 32, 16, 8))
    return pl.pallas_call(
        _mla_o_kernel,
        grid=(m // tm,),
        in_specs=[pl.BlockSpec((tm, N_HEADS * KV_LORA), lambda i: (i, 0)),
                  pl.BlockSpec(w_uv.shape, lambda i: (0, 0, 0))],
        out_specs=pl.BlockSpec((tm, BRANCH), lambda i: (i, 0)),
        out_shape=jax.ShapeDtypeStruct((m, BRANCH), F32),
        compiler_params=_cparams("parallel"),
        name="mla_o",
    )(o_lat, w_uv)


def _diff_lambda(lam_ref):
    lp = lam_ref[...]
    lam_init = 0.8 - 0.6 * math.exp(-0.3 * D_LAYER)
    a = jnp.sum(lp[0:1] * lp[1:2], axis=-1, keepdims=True)
    b = jnp.sum(lp[2:3] * lp[3:4], axis=-1, keepdims=True)
    return jnp.exp(a) - jnp.exp(b) + lam_init, lam_init


def _diff_finish(acc, l, lam_ref, subln_ref):
    half = acc.shape[0] // 2
    lam, lam_init = _diff_lambda(lam_ref)
    o = acc / l
    o = o[:half] - lam * o[half:]
    y = o * lax.rsqrt(jnp.mean(o * o, axis=-1, keepdims=True) + EPS) * subln_ref[...]
    return y * (1.0 - lam_init)


def _split_components(q):
    lane = lax.broadcasted_iota(jnp.int32, q.shape, q.ndim - 1)
    return jnp.where(lane < DH_D, q, 0.0), jnp.where(lane < DH_D, 0.0, q)


def _diff_prompt_kernel(qi_ref, ki_ref, q_ref, k_ref, v_ref, bias_ref, lam_ref, subln_ref, o_ref,
                        qs_ref, m_ref, l_ref, acc_ref, *, g, tq, tk, scale):
    qi, ki = qi_ref[pl.program_id(2)], ki_ref[pl.program_id(2)]
    last = (qi * tq + tq - 1) // tk

    @pl.when(ki == 0)
    def _():
        q = q_ref[...] * scale
        ones, twos = [], []
        for i in range(g):
            a, b = _split_components(q[:, i * HEAD_DIM:(i + 1) * HEAD_DIM])
            ones.append(a)
            twos.append(b)
        qs_ref[...] = jnp.concatenate(ones + twos, axis=0).astype(BF16)
        m_ref[...] = jnp.full(m_ref.shape, NEG, F32)
        l_ref[...] = jnp.zeros(l_ref.shape, F32)
        acc_ref[...] = jnp.zeros(acc_ref.shape, F32)

    st = _dot_t(k_ref[...].astype(BF16), qs_ref[...])
    tiles = [pltpu.roll(jnp.broadcast_to(bias_ref[0, 0, i], (tk, tq + tk)), 0, 1, stride=1, stride_axis=0)[:, :tq]
             for i in range(g)]
    st = st + jnp.concatenate(tiles * 2, axis=1)
    m_old = m_ref[...]
    m_new = jnp.maximum(m_old, jnp.max(st, axis=0, keepdims=True))
    alpha = jnp.exp(m_old - m_new)
    p = jnp.exp(st - m_new)
    l_ref[...] = alpha * l_ref[...] + jnp.sum(p, axis=0, keepdims=True)
    vt = v_ref[...].T.astype(BF16)
    acc_ref[...] = alpha * acc_ref[...] + _dot(vt, p.astype(BF16))
    m_ref[...] = m_new

    @pl.when(ki == last)
    def _():
        ot = acc_ref[...] / l_ref[...]
        half = ot.shape[1] // 2
        lam, lam_init = _diff_lambda(lam_ref)
        d = ot[:, :half] - lam * ot[:, half:]
        y = d * lax.rsqrt(jnp.mean(d * d, axis=0, keepdims=True) + EPS) * subln_ref[...] * (1.0 - lam_init)
        yt = y.T
        for i in range(g):
            o_ref[:, i * HEAD_DIM:(i + 1) * HEAD_DIM] = yt[i * tq:(i + 1) * tq].astype(o_ref.dtype)


def _diff_prompt(z, q_col, k_col, v_col, n_seq, seq_len, bias, lam_p, subln, tq, tk):
    g = N_HEADS // KV_D
    nq, nk, ratio = seq_len // tq, seq_len // tk, tk // tq
    qw = g * HEAD_DIM
    qc, kc, vc = q_col // qw, k_col // HEAD_DIM, v_col // HEAD_DIM
    pairs = [(i, j) for i in range(nq) for j in range((i * tq + tq - 1) // tk + 1)]
    qi_tab = jnp.asarray([p[0] for p in pairs], jnp.int32)
    ki_tab = jnp.asarray([p[1] for p in pairs], jnp.int32)
    rows = 2 * g * tq
    kv_spec = lambda c0: pl.BlockSpec((tk, HEAD_DIM), lambda b, h, p, qt, kt: (b * nk + kt[p], c0 + h))
    full = lambda a: pl.BlockSpec(a.shape, lambda b, h, p, qt, kt: (0,) * a.ndim)
    subln_col = subln.reshape(HEAD_DIM, 1)
    grid_spec = pltpu.PrefetchScalarGridSpec(
        num_scalar_prefetch=2,
        grid=(n_seq, KV_D, len(pairs)),
        in_specs=[pl.BlockSpec((tq, qw), lambda b, h, p, qt, kt: (b * nq + qt[p], qc + h)),
                  kv_spec(kc), kv_spec(vc),
                  pl.BlockSpec((1, 1, g, 1, tq + tk), lambda b, h, p, qt, kt: (qt[p] - ratio * kt[p], h, 0, 0, 0)),
                  full(lam_p), full(subln_col)],
        out_specs=pl.BlockSpec((tq, qw), lambda b, h, p, qt, kt: (b * nq + qt[p], h)),
        scratch_shapes=[pltpu.VMEM((rows, HEAD_DIM), BF16), pltpu.VMEM((1, rows), F32),
                        pltpu.VMEM((1, rows), F32), pltpu.VMEM((HEAD_DIM, rows), F32)],
    )
    return pl.pallas_call(
        functools.partial(_diff_prompt_kernel, g=g, tq=tq, tk=tk, scale=DH_D ** -0.5),
        grid_spec=grid_spec,
        out_shape=jax.ShapeDtypeStruct((n_seq * seq_len, BRANCH), BF16),
        compiler_params=_cparams("parallel", "parallel", "arbitrary"),
        name="diff_prompt",
    )(qi_tab, ki_tab, z, z, z, bias, lam_p, subln_col)


def _diff_sample_kernel(pt_ref, q_ref, *refs, pps, scale):
    pages = refs[:pps]
    kn_ref, vn_ref, bp_ref, bn_ref, lam_ref, subln_ref, o_ref, m_ref, l_ref, acc_ref = refs[pps:]
    p_idx = pl.program_id(1)
    rw = 2 * KV_D

    @pl.when(p_idx == 0)
    def _():
        m_ref[...] = jnp.full(m_ref.shape, NEG, F32)
        l_ref[...] = jnp.zeros(l_ref.shape, F32)
        acc_ref[...] = jnp.zeros(acc_ref.shape, F32)

    q = (q_ref[0] * scale).astype(BF16)

    def update(k, v, bias):
        s = jnp.einsum("hrd,hkd->hrk", q, k, preferred_element_type=F32) + jnp.concatenate([bias, bias], axis=1)
        m_old = m_ref[...]
        m_new = jnp.maximum(m_old, jnp.max(s, axis=-1, keepdims=True))
        alpha = jnp.exp(m_old - m_new)
        p = jnp.exp(s - m_new)
        l_ref[...] = alpha * l_ref[...] + jnp.sum(p, axis=-1, keepdims=True)
        pv = jnp.einsum("hrk,hkd->hrd", p.astype(BF16), v, preferred_element_type=F32)
        acc_ref[...] = alpha * acc_ref[...] + pv
        m_ref[...] = m_new

    def head_rows(first_row):
        return jnp.stack([jnp.concatenate([r[pl.ds(first_row + h, PAGE, stride=rw), :] for r in pages], axis=0)
                          for h in range(KV_D)]).astype(BF16)

    update(head_rows(0), head_rows(KV_D), bp_ref[...])

    @pl.when(p_idx == pl.num_programs(1) - 1)
    def _():
        update(kn_ref[0].astype(BF16), vn_ref[0].astype(BF16), bn_ref[...])
        for h in range(KV_D):
            o_ref[0, h] = _diff_finish(acc_ref[h], l_ref[h], lam_ref, subln_ref)


def _diff_sample(page_table, q, pool_flat, k_new, v_new, bias_past, bias_new, lam_p, subln):
    n_seq, _, rows2, dh = q.shape
    rows = rows2 // 2
    n_pages = page_table.shape[1]
    pps = _pick(n_pages, (PAGES_PER_STEP, 4, 2, 1))
    n_pad = k_new.shape[2]
    rw = 2 * KV_D
    per_seq = lambda shape: pl.BlockSpec(shape, lambda b, p, pt: (b,) + (0,) * (len(shape) - 1))
    full = lambda shape: pl.BlockSpec(shape, lambda b, p, pt: (0,) * len(shape))
    grid_spec = pltpu.PrefetchScalarGridSpec(
        num_scalar_prefetch=1,
        grid=(n_seq, n_pages // pps),
        in_specs=[per_seq((1, KV_D, rows2, dh))]
        + [pl.BlockSpec((PAGE * rw, dh), functools.partial(lambda b, p, pt, i: (pt[b, p * pps + i], 0), i=i))
           for i in range(pps)]
        + [per_seq((1, KV_D, n_pad, dh)), per_seq((1, KV_D, n_pad, dh)),
           pl.BlockSpec((KV_D, rows, pps * PAGE), lambda b, p, pt: (0, 0, p)),
           full((KV_D, rows, n_pad)), full(lam_p.shape), full(subln.shape)],
        out_specs=per_seq((1, KV_D, rows, dh)),
        scratch_shapes=[pltpu.VMEM((KV_D, rows2, 1), F32), pltpu.VMEM((KV_D, rows2, 1), F32),
                        pltpu.VMEM((KV_D, rows2, dh), F32)],
    )
    return pl.pallas_call(
        functools.partial(_diff_sample_kernel, pps=pps, scale=DH_D ** -0.5),
        grid_spec=grid_spec,
        out_shape=jax.ShapeDtypeStruct((n_seq, KV_D, rows, dh), F32),
        compiler_params=_cparams("parallel", "arbitrary"),
        name="diff_sample",
    )(page_table, q, *([pool_flat] * pps), k_new, v_new, bias_past, bias_new, lam_p, subln)


def _heads_to_rows(x, n_seq, n_new, n_kv):
    g = N_HEADS // n_kv
    x = x.reshape(n_seq, n_new, n_kv, g, HEAD_DIM)
    return jnp.transpose(x, (0, 2, 3, 1, 4)).reshape(n_seq, n_kv, g * n_new, HEAD_DIM)


def _rows_to_heads(x, n_seq, n_new, n_kv):
    g = N_HEADS // n_kv
    x = x.reshape(n_seq, n_kv, g, n_new, HEAD_DIM)
    return jnp.transpose(x, (0, 3, 1, 2, 4)).reshape(n_seq * n_new, BRANCH)


def _new_kv(kv_cols, n_seq, n_new, n_kv, n_pad):
    kv = kv_cols.reshape(n_seq, n_new, 2, n_kv, HEAD_DIM)
    pad = lambda a: jnp.pad(jnp.transpose(a, (0, 2, 1, 3)), ((0, 0), (0, 0), (0, n_pad - n_new), (0, 0)))
    return kv_cols.reshape(n_seq * n_new * 2 * n_kv, HEAD_DIM), pad(kv[:, :, 0]), pad(kv[:, :, 1])


def _gate_first(w, gate_col):
    return jnp.concatenate([w[:, gate_col:gate_col + BRANCH], w[:, :gate_col], w[:, gate_col + BRANCH:]], axis=1)


def _swap_halves(w):
    half = w.shape[-1] // 2
    return jnp.concatenate([w[..., half:], w[..., :half]], axis=-1)


def _rope_tables(pos):
    half = ROPE_DIM // 2
    inv = ROPE_THETA ** (-jnp.arange(half, dtype=F32) / half)
    ang = pos.astype(F32)[:, None] * inv[None, :]
    cos, sin = jnp.cos(ang), jnp.sin(ang)
    return jnp.concatenate([cos, cos, cos, cos], axis=1), jnp.concatenate([-sin, sin, -sin, sin], axis=1)


def kernel(x_prompt, x_sample, cache_a_kv, cache_b_lat, cache_b_kpe, cache_c_kv1, cache_c_kv2, cache_c_kv3, cache_d_kv, page_table, rel_bias, ln_gain, final_gain, a_w_in, a_sink, a_w_out, b_w_in, b_q_norm, b_kv_norm, b_w_uq, b_w_uk, b_w_uv, b_w_out, c_w_in, c_w_out, d_w_in, d_lambda, d_subln, d_w_out):
    nb, t, d = x_prompt.shape
    ns, n_new, _ = x_sample.shape
    assert d == D_MODEL and ln_gain.shape[0] == 4 and t % (max(C_DILATIONS) * BLOCK) == 0
    n_past = page_table.shape[1] * PAGE
    mp, ms = nb * t, ns * n_new
    n_pad = 8
    hp, hs = x_prompt.reshape(mp, d), x_sample.reshape(ms, d)
    f, f_rev = _t5_table(rel_bias, max(n_past + n_new, t) + n_pad)
    bf = lambda w: w.astype(BF16)

    nq, nk = BRANCH, KV_A * HEAD_DIM
    w_in = bf(_gate_first(a_w_in[0], nq + 2 * nk))
    w_out = bf(a_w_out[0])
    zp, zs = _rms_proj(hp, ln_gain[0], w_in), _rms_proj(hs, ln_gain[0], w_in)
    band_bias = {dil: _band_bias(f_rev, dil) for dil in C_DILATIONS}
    o_p = _band_attention(zp, BRANCH, BRANCH + nq, BRANCH + nq + nk, nb, t, KV_A,
                          _band_bias_t(f, 1), _sink_lanes(a_sink[0], KV_A, BLOCK), False)
    a_kv_prompt = zp[:, BRANCH + nq:].reshape(nb, t, 2, KV_A, HEAD_DIM)[:, -min(WIN_A, t):][None]
    n_buf = cache_a_kv.shape[2]
    new_flat, k_new, v_new = _new_kv(zs[:, BRANCH + nq:], ns, n_new, KV_A, n_pad)
    bias_buf, bias_new = _decode_bias(f_rev, n_buf, n_new, WIN_A, 1, KV_A, n_pad)
    o_s, cache = _window_decode(_heads_to_rows(zs[:, BRANCH:BRANCH + nq], ns, n_new, KV_A),
                                cache_a_kv[0].reshape(-1, HEAD_DIM), new_flat, k_new, v_new, bias_buf, bias_new,
                                _sink_rows(a_sink[0], KV_A, n_new), KV_A, n_buf, n_new, False)
    a_kv_sample = cache.reshape(cache_a_kv.shape)
    hp = _gated_out(hp, zp, [o_p], [], w_out)
    hs = _gated_out(hs, zs, [_rows_to_heads(o_s, ns, n_new, KV_A)], [], w_out)

    w = b_w_in[0]
    kpe_w = w[:, Q_LORA + KV_LORA:Q_LORA + KV_LORA + ROPE_DIM]
    w_in = bf(jnp.concatenate([w[:, Q_LORA + KV_LORA + ROPE_DIM:], w[:, :Q_LORA + KV_LORA], kpe_w, _swap_halves(kpe_w)], axis=1))
    w_out = bf(b_w_out[0])
    uq = b_w_uq[0]
    w_nope = bf(uq[:, :, :NOPE_DIM].reshape(Q_LORA, N_HEADS * NOPE_DIM))
    pad_pe = lambda a: jnp.pad(a, ((0, 0), (0, 0), (0, 128 - ROPE_DIM))).reshape(Q_LORA, N_HEADS * 128)
    w_pe, w_pe_sw = bf(pad_pe(uq[:, :, NOPE_DIM:])), bf(pad_pe(_swap_halves(uq[:, :, NOPE_DIM:])))
    w_ukt = bf(jnp.transpose(b_w_uk[0], (1, 2, 0)))
    w_uv = bf(jnp.transpose(b_w_uv[0], (1, 0, 2)))
    zp, zs = _rms_proj(hp, ln_gain[1], w_in), _rms_proj(hs, ln_gain[1], w_in)
    cos_p, sin_p = _rope_tables(jnp.arange(t))
    cos_s, sin_s = _rope_tables(n_past + jnp.arange(n_new))
    tmq = _pick(mp, (256, 128, 64, 32, 16, 8))
    assert t % tmq == 0
    ql_p, qp_p, lat_p, kpe_p, latb_p, kpeb_p, latt_p = _mla_q(zp, cos_p, sin_p, b_q_norm[0], b_kv_norm[0],
                                                               w_nope, w_pe, w_pe_sw, w_ukt, t // tmq)
    ql_s, qp_s, lat_s, kpe_s, _, _, _ = _mla_q(zs, jnp.tile(cos_s, (ns, 1)), jnp.tile(sin_s, (ns, 1)), b_q_norm[0],
                                                b_kv_norm[0], w_nope, w_pe, w_pe_sw, w_ukt, ms)
    o_p = _mla_prompt(ql_p, qp_p, latb_p, latt_p, kpeb_p, w_uv, nb, t)
    to_rows = lambda a, w_: jnp.transpose(a.reshape(ns, n_new, N_HEADS, w_), (0, 2, 1, 3)).reshape(ns, N_HEADS * n_new, w_)
    qls = to_rows(ql_s, KV_LORA)
    qps = to_rows(qp_s.reshape(ms, N_HEADS, 128)[:, :, :ROPE_DIM].reshape(ms, N_HEADS * ROPE_DIM), ROPE_DIM)
    padn = lambda a: jnp.pad(a.reshape(ns, n_new, -1), ((0, 0), (0, n_pad - n_new), (0, 0)))
    ii = jnp.arange(N_HEADS * n_new)[:, None] % n_new
    mask_new = jnp.where(jnp.arange(n_pad)[None, :] <= ii, 0.0, NEG).astype(F32)
    ol_s = _mla_sample(page_table, qls, qps, cache_b_lat[0], jnp.swapaxes(cache_b_kpe[0], -1, -2),
                       padn(lat_s), padn(kpe_s), mask_new)
    ol_s = jnp.transpose(ol_s.reshape(ns, N_HEADS, n_new, KV_LORA), (0, 2, 1, 3)).reshape(ms, N_HEADS * KV_LORA)
    hp = _gated_out(hp, zp, [o_p], [], w_out)
    hs = _gated_out(hs, zs, [_mla_o(ol_s, w_uv)], [], w_out)
    b_lat_prompt, b_lat_sample = lat_p.reshape(1, nb, t, KV_LORA), lat_s.reshape(1, ns, n_new, KV_LORA)
    b_kpe_prompt, b_kpe_sample = kpe_p.reshape(1, nb, t, ROPE_DIM), kpe_s.reshape(1, ns, n_new, ROPE_DIM)

    n_grp = len(C_WINDOWS)
    gw = BRANCH + 2 * KV_C * HEAD_DIM
    w_in = bf(_gate_first(c_w_in[0], n_grp * gw))
    w_out = bf(c_w_out[0])
    zp, zs = _rms_proj(hp, ln_gain[2], w_in), _rms_proj(hs, ln_gain[2], w_in)
    outs_p, lses_p, outs_s, lses_s, c_prompt, c_sample = [], [], [], [], [], []
    for gi, (win, dil, buf) in enumerate(zip(C_WINDOWS, C_DILATIONS, (cache_c_kv1, cache_c_kv2, cache_c_kv3))):
        c0 = BRANCH + gi * gw
        kv_p = zp[:, c0 + BRANCH:c0 + gw]
        c_prompt.append(kv_p.reshape(nb, t, 2, KV_C, HEAD_DIM)[:, -min(win, t):][None])
        assert win // dil == BLOCK
        bias = band_bias[dil]
        cols = (c0, c0 + BRANCH, c0 + BRANCH + KV_C * HEAD_DIM)
        if dil == 1:
            o, lse = _band_attention(zp, *cols, nb, t, KV_C, bias, None, True)
        else:
            o, lse = _dilated_band_attention(zp, *cols, nb, t, KV_C, dil, bias)
        outs_p.append(o)
        lses_p.append(lse)
        n_buf = buf.shape[2]
        new_flat, k_new, v_new = _new_kv(zs[:, c0 + BRANCH:c0 + gw], ns, n_new, KV_C, n_pad)
        bias_buf, bias_new = _decode_bias(f_rev, n_buf, n_new, win, dil, KV_C, n_pad)
        o, lse, cache = _window_decode(_heads_to_rows(zs[:, c0:c0 + BRANCH], ns, n_new, KV_C),
                                       buf[0].reshape(-1, HEAD_DIM), new_flat, k_new, v_new, bias_buf, bias_new,
                                       None, KV_C, n_buf, n_new, True)
        outs_s.append(_rows_to_heads(o, ns, n_new, KV_C))
        lses_s.append(_rows_to_heads(lse, ns, n_new, KV_C))
        c_sample.append(cache.reshape(buf.shape))
    hp = _gated_out(hp, zp, outs_p, lses_p, w_out)
    hs = _gated_out(hs, zs, outs_s, lses_s, w_out)

    nkv = 2 * KV_D * HEAD_DIM
    w_in = bf(_gate_first(d_w_in[0], BRANCH + nkv))
    w_out = bf(d_w_out[0])
    zp, zs = _rms_proj(hp, ln_gain[3], w_in), _rms_proj(hs, ln_gain[3], w_in)
    lam_p, subln = d_lambda[0].astype(F32), d_subln[0].reshape(1, HEAD_DIM).astype(F32)
    tq = _pick(t, (256, 128))
    tk = _pick(t, (2 * tq, tq))
    gen = _causal_bias_rows(f[:, :t], tq, tk).reshape(t // tq, KV_D, N_HEADS // KV_D, 1, tq + tk)
    o_p = _diff_prompt(zp, BRANCH, 2 * BRANCH, 2 * BRANCH + KV_D * HEAD_DIM, nb, t, gen, lam_p, subln, tq, tk)
    d_kv_prompt = zp[:, 2 * BRANCH:].reshape(1, nb, t, 2, KV_D, HEAD_DIM)
    d_kv_sample = zs[:, 2 * BRANCH:].reshape(1, ns, n_new, 2, KV_D, HEAD_DIM)
    _, k_new, v_new = _new_kv(zs[:, 2 * BRANCH:], ns, n_new, KV_D, n_pad)
    bias_past, bias_new = _decode_bias(f_rev, n_past, n_new, n_past + n_new, 1, KV_D, n_pad)
    q = _heads_to_rows(zs[:, BRANCH:2 * BRANCH], ns, n_new, KV_D)
    q = jnp.concatenate(_split_components(q), axis=2)
    o_s = _diff_sample(page_table, q, cache_d_kv[0].reshape(-1, HEAD_DIM), k_new, v_new, bias_past, bias_new, lam_p, subln)
    y_prompt = _gated_out(hp, zp, [o_p], [], w_out, final_gain).reshape(nb, t, d)
    y_sample = _gated_out(hs, zs, [_rows_to_heads(o_s, ns, n_new, KV_D)], [], w_out, final_gain).reshape(ns, n_new, d)
    return (y_prompt, y_sample, a_kv_prompt, a_kv_sample, b_lat_prompt, b_lat_sample, b_kpe_prompt, b_kpe_sample,
            c_prompt[0], c_sample[0], c_prompt[1], c_sample[1], c_prompt[2], c_sample[2], d_kv_prompt, d_kv_sample)
```
